```python
import jax, jax.numpy as jnp
from jax import lax
import numpy as np

D_MODEL = 2048
BATCH = 8
SEQ = 4096
DEPTH = 2

N_MIXERS = 2
NORM_EPS = 1e-6
N_ML_LAYERS = (DEPTH + 1) // 2
N_SSD_LAYERS = DEPTH // 2

ML_HEADS = 4
ML_V_DIM = D_MODEL // ML_HEADS
ML_QK_DIM = ML_V_DIM // 2
ML_QK_TOT = ML_HEADS * ML_QK_DIM
ML_V_TOT = ML_HEADS * ML_V_DIM
ML_SPLITS = (ML_QK_TOT, 2 * ML_QK_TOT, 2 * ML_QK_TOT + ML_V_TOT,
             2 * ML_QK_TOT + 2 * ML_V_TOT, 2 * ML_QK_TOT + 2 * ML_V_TOT + ML_HEADS)
ML_IN_DIM = 2 * ML_QK_TOT + 2 * ML_V_TOT + 2 * ML_HEADS
ML_CHUNK = 64

SSM_D_INNER = 2 * D_MODEL
SSM_HEAD_DIM = 64
SSM_HEADS = SSM_D_INNER // SSM_HEAD_DIM
SSM_GROUPS = 8
SSM_HEADS_PER_GROUP = SSM_HEADS // SSM_GROUPS
SSM_STATE = 128
SSM_CONV = 4
SSM_CHUNK = 128
SSM_CONV_DIM = SSM_D_INNER + 2 * SSM_GROUPS * SSM_STATE
SSM_IN_DIM = SSM_D_INNER + SSM_CONV_DIM + SSM_HEADS

MOE_GROUPS = 8
MOE_EXPERTS_PER_GROUP = 8
MOE_EXPERTS = MOE_GROUPS * MOE_EXPERTS_PER_GROUP
MOE_TOP_K = 2
MOE_D_FF = 512
MOE_BLOCK = 128

kernel_name = "hybrid_mlstm_ssd_hmoe_trunk"


def rms_norm(x, w):
    xf = x.astype(jnp.float32)
    y = xf * lax.rsqrt(jnp.mean(xf * xf, axis=-1, keepdims=True) + NORM_EPS)
    return (y * w.astype(jnp.float32)).astype(x.dtype)


def mlstm_mixer(h, w_in, b_i, b_f, norm_w, w_out):
    B_, S_, _ = h.shape
    nc = S_ // ML_CHUNK
    proj = jnp.einsum('bsd,de->bse', h, w_in).astype(jnp.float32)
    q, k, v, o, ig, fg = jnp.split(proj, list(ML_SPLITS), axis=-1)
    q = q * (ML_QK_DIM ** -0.5)
    ig = ig + b_i.astype(jnp.float32)
    lf = jax.nn.log_sigmoid(fg + b_f.astype(jnp.float32))

    def to_chunks(t, dim):
        return t.reshape(B_, nc, ML_CHUNK, ML_HEADS, dim).transpose(1, 0, 3, 2, 4)

    def gate_chunks(t):
        return t.reshape(B_, nc, ML_CHUNK, ML_HEADS).transpose(1, 0, 3, 2)

    qc = to_chunks(q.reshape(B_, S_, ML_HEADS, ML_QK_DIM), ML_QK_DIM)
    kc = to_chunks(k.reshape(B_, S_, ML_HEADS, ML_QK_DIM), ML_QK_DIM)
    vc = to_chunks(v.reshape(B_, S_, ML_HEADS, ML_V_DIM), ML_V_DIM)
    icc = gate_chunks(ig)
    lfc = gate_chunks(lf)
    causal = jnp.tril(jnp.ones((ML_CHUNK, ML_CHUNK), dtype=bool))

    def step(carry, inp):
        C, n, m = carry
        qb, kb, vb, ib, fb = inp
        b = jnp.cumsum(fb, axis=-1)
        logd = b[..., :, None] - b[..., None, :] + ib[..., None, :]
        logd = jnp.where(causal, logd, -jnp.inf)
        inter = b + m[..., None]
        m_t = jnp.maximum(inter, jnp.max(logd, axis=-1))
        s = jnp.einsum('bhtk,bhsk->bhts', qb, kb) * jnp.exp(logd - m_t[..., None])
        sc = jnp.exp(inter - m_t)
        num = jnp.einsum('bhts,bhsv->bhtv', s, vb) + sc[..., None] * jnp.einsum('bhtk,bhkv->bhtv', qb, C)
        den = jnp.sum(s, axis=-1) + sc * jnp.einsum('bhtk,bhk->bht', qb, n)
        hout = num / jnp.maximum(jnp.abs(den), jnp.exp(-m_t))[..., None]
        b_last = b[..., -1]
        g = b_last[..., None] - b + ib
        m_new = jnp.maximum(b_last + m, jnp.max(g, axis=-1))
        ws = jnp.exp(g - m_new[..., None])
        dec = jnp.exp(b_last + m - m_new)
        C_new = dec[..., None, None] * C + jnp.einsum('bhsk,bhsv->bhkv', ws[..., None] * kb, vb)
        n_new = dec[..., None] * n + jnp.einsum('bhs,bhsk->bhk', ws, kb)
        return (C_new, n_new, m_new), hout

    init = (jnp.zeros((B_, ML_HEADS, ML_QK_DIM, ML_V_DIM), jnp.float32),
            jnp.zeros((B_, ML_HEADS, ML_QK_DIM), jnp.float32),
            jnp.zeros((B_, ML_HEADS), jnp.float32))
    _, hs = lax.scan(step, init, (qc, kc, vc, icc, lfc))
    hs = hs.transpose(1, 0, 3, 2, 4).reshape(B_, S_, ML_HEADS, ML_V_DIM)
    hs = hs * lax.rsqrt(jnp.mean(hs * hs, axis=-1, keepdims=True) + NORM_EPS)
    hs = hs * norm_w.astype(jnp.float32).reshape(ML_HEADS, ML_V_DIM)
    hs = hs.reshape(B_, S_, ML_V_TOT) * jax.nn.sigmoid(o)
    return jnp.einsum('bse,ed->bsd', hs.astype(h.dtype), w_out).astype(h.dtype)


def causal_depthwise_conv(u, w, b):
    c = u.shape[-1]
    out = lax.conv_general_dilated(u, w.astype(u.dtype)[:, None, :], window_strides=(1,),
                                   padding=[(SSM_CONV - 1, 0)],
                                   dimension_numbers=('NWC', 'WIO', 'NWC'),
                                   feature_group_count=c)
    return out + b.astype(u.dtype)


def ssd_mixer(h, w_in, conv_w, conv_b, dt_bias, a_log, d_skip, norm_w, w_out):
    B_, S_, _ = h.shape
    nc = S_ // SSM_CHUNK
    G, HG, P, N, L = SSM_GROUPS, SSM_HEADS_PER_GROUP, SSM_HEAD_DIM, SSM_STATE, SSM_CHUNK
    proj = jnp.einsum('bsd,de->bse', h, w_in).astype(jnp.float32)
    z, xbc, dt = jnp.split(proj, [SSM_D_INNER, SSM_D_INNER + SSM_CONV_DIM], axis=-1)
    xbc = jax.nn.silu(causal_depthwise_conv(xbc, conv_w, conv_b))
    xs, bm, cm = jnp.split(xbc, [SSM_D_INNER, SSM_D_INNER + G * N], axis=-1)
    dt = jax.nn.softplus(dt + dt_bias.astype(jnp.float32))
    a = -jnp.exp(a_log.astype(jnp.float32))
    da = dt * a

    xh = xs.reshape(B_, S_, SSM_HEADS, P)
    xc = jnp.moveaxis(xh.reshape(B_, nc, L, G, HG, P), 1, 0)
    dtc = jnp.moveaxis(dt.reshape(B_, nc, L, G, HG), 1, 0)
    dac = jnp.moveaxis(da.reshape(B_, nc, L, G, HG), 1, 0)
    bc = jnp.moveaxis(bm.reshape(B_, nc, L, G, N), 1, 0)
    cc = jnp.moveaxis(cm.reshape(B_, nc, L, G, N), 1, 0)
    causal = jnp.tril(jnp.ones((L, L), dtype=bool))

    def step(state, inp):
        xb, dtb, dab, bb, cb_ = inp
        cum = jnp.cumsum(dab, axis=1)
        cum_t = cum.transpose(0, 2, 3, 1)
        seg = cum_t[..., :, None] - cum_t[..., None, :]
        decay = jnp.exp(jnp.where(causal, seg, -jnp.inf))
        cbm = jnp.einsum('btgn,bsgn->bgts', cb_, bb)
        wts = cbm[:, :, None] * decay * dtb.transpose(0, 2, 3, 1)[..., None, :]
        y_diag = jnp.einsum('bghts,bsghp->btghp', wts, xb)
        y_off = jnp.einsum('btgn,bghpn->btghp', cb_, state) * jnp.exp(cum)[..., None]
        to_end = jnp.exp(cum[:, -1:] - cum) * dtb
        new_state = state * jnp.exp(cum[:, -1])[..., None, None] + \
            jnp.einsum('bsgn,bsghp->bghpn', bb, to_end[..., None] * xb)
        return new_state, y_diag + y_off

    init = jnp.zeros((B_, G, HG, P, N), jnp.float32)
    _, ys = lax.scan(step, init, (xc, dtc, dac, bc, cc))
    y = jnp.moveaxis(ys, 0, 1).reshape(B_, S_, SSM_HEADS, P)
    y = y + d_skip.astype(jnp.float32)[:, None] * xh
    y = y.reshape(B_, S_, SSM_D_INNER) * jax.nn.silu(z)
    yg = y.reshape(B_, S_, G, SSM_D_INNER // G)
    yg = yg * lax.rsqrt(jnp.mean(yg * yg, axis=-1, keepdims=True) + NORM_EPS)
    y = yg.reshape(B_, S_, SSM_D_INNER) * norm_w.astype(jnp.float32)
    return jnp.einsum('bse,ed->bsd', y.astype(h.dtype), w_out).astype(h.dtype)


def hier_moe(h, w_group, b_group, w_expert, b_expert, w_gate, w_up, w_down):
    B_, S_, D = h.shape
    T = B_ * S_
    A = T * MOE_TOP_K
    xt = h.reshape(T, D)
    g_prob = jax.nn.softmax((xt @ w_group + b_group).astype(jnp.float32), axis=-1)
    g_w, g_idx = lax.top_k(g_prob, 1)
    e_logits = (xt @ w_expert + b_expert).astype(jnp.float32).reshape(T, MOE_GROUPS, MOE_EXPERTS_PER_GROUP)
    e_sel = jnp.take_along_axis(e_logits, g_idx[:, :, None], axis=1)[:, 0]
    e_w, e_loc = lax.top_k(jax.nn.softmax(e_sel, axis=-1), MOE_TOP_K)
    e_w = e_w / jnp.sum(e_w, axis=-1, keepdims=True)
    gate = g_w * e_w
    eid = g_idx * MOE_EXPERTS_PER_GROUP + e_loc

    flat_e = eid.reshape(A)
    flat_w = gate.reshape(A)
    flat_tok = jnp.repeat(jnp.arange(T, dtype=jnp.int32), MOE_TOP_K)
    order = jnp.argsort(flat_e)
    se, stok, sw = flat_e[order], flat_tok[order], flat_w[order]
    counts = jnp.bincount(flat_e, length=MOE_EXPERTS)
    starts = jnp.cumsum(counts) - counts
    padded = (counts + MOE_BLOCK - 1) // MOE_BLOCK * MOE_BLOCK
    pends = jnp.cumsum(padded)
    pstarts = pends - padded
    dest = pstarts[se] + jnp.arange(A, dtype=jnp.int32) - starts[se]
    n_rows = A + MOE_EXPERTS * MOE_BLOCK
    n_blocks = n_rows // MOE_BLOCK
    rows_tok = jnp.full((n_rows,), T, jnp.int32).at[dest].set(stok)
    rows_w = jnp.zeros((n_rows,), jnp.float32).at[dest].set(sw)
    blk_e = jnp.minimum(jnp.searchsorted(pends, jnp.arange(n_blocks) * MOE_BLOCK, side='right'),
                        MOE_EXPERTS - 1)
    x_pad = jnp.concatenate([xt, jnp.zeros((1, D), xt.dtype)], axis=0)
    xb = x_pad[rows_tok].reshape(n_blocks, MOE_BLOCK, D)

    def expert_block(args):
        xblk, e = args
        hid = jax.nn.silu(xblk @ w_gate[e]) * (xblk @ w_up[e])
        return hid @ w_down[e]

    yb = lax.map(expert_block, (xb, blk_e)).reshape(n_rows, D)
    yb = yb * rows_w[:, None].astype(yb.dtype)
    out = jax.ops.segment_sum(yb, rows_tok, num_segments=T + 1)[:T]
    return out.reshape(B_, S_, D).astype(h.dtype)


def setup_inputs(seed: int = 0) -> dict:
    key = jax.random.key(seed)
    ks = jax.random.split(key, 26)
    f32 = jnp.float32
    nrm = lambda k, shp, s: jax.random.normal(k, shp, f32) * s
    dt_u = jax.random.uniform(ks[12], (N_SSD_LAYERS, SSM_HEADS), f32)
    dt0 = jnp.exp(dt_u * (np.log(0.1) - np.log(0.001)) + np.log(0.001)).astype(f32)
    return {
        "x": nrm(ks[0], (BATCH, SEQ, D_MODEL), 1.0),
        "norm_mix_w": 1.0 + nrm(ks[1], (DEPTH, D_MODEL), 0.02),
        "norm_ffn_w": 1.0 + nrm(ks[2], (DEPTH, D_MODEL), 0.02),
        "ml_w_in": nrm(ks[3], (N_ML_LAYERS, D_MODEL, ML_IN_DIM), D_MODEL ** -0.5),
        "ml_b_i": nrm(ks[4], (N_ML_LAYERS, ML_HEADS), 0.1),
        "ml_b_f": jnp.linspace(3.0, 6.0, ML_HEADS, dtype=f32)[None] + nrm(ks[5], (N_ML_LAYERS, ML_HEADS), 0.1),
        "ml_norm_w": 1.0 + nrm(ks[6], (N_ML_LAYERS, ML_V_TOT), 0.02),
        "ml_w_out": nrm(ks[7], (N_ML_LAYERS, ML_V_TOT, D_MODEL), ML_V_TOT ** -0.5),
        "ssd_w_in": nrm(ks[8], (N_SSD_LAYERS, D_MODEL, SSM_IN_DIM), D_MODEL ** -0.5),
        "ssd_conv_w": nrm(ks[9], (N_SSD_LAYERS, SSM_CONV, SSM_CONV_DIM), SSM_CONV ** -0.5),
        "ssd_conv_b": nrm(ks[10], (N_SSD_LAYERS, SSM_CONV_DIM), 0.02),
        "ssd_dt_bias": dt0 + jnp.log(-jnp.expm1(-dt0)),
        "ssd_a_log": jnp.log(jax.random.uniform(ks[11], (N_SSD_LAYERS, SSM_HEADS), f32, 1.0, 16.0)),
        "ssd_d": 1.0 + nrm(ks[13], (N_SSD_LAYERS, SSM_HEADS), 0.1),
        "ssd_norm_w": 1.0 + nrm(ks[14], (N_SSD_LAYERS, SSM_D_INNER), 0.02),
        "ssd_w_out": nrm(ks[15], (N_SSD_LAYERS, SSM_D_INNER, D_MODEL), SSM_D_INNER ** -0.5),
        "moe_w_group": nrm(ks[16], (DEPTH, D_MODEL, MOE_GROUPS), D_MODEL ** -0.5),
        "moe_b_group": nrm(ks[17], (DEPTH, MOE_GROUPS), 0.01),
        "moe_w_expert": nrm(ks[18], (DEPTH, D_MODEL, MOE_EXPERTS), D_MODEL ** -0.5),
        "moe_b_expert": nrm(ks[19], (DEPTH, MOE_EXPERTS), 0.01),
        "moe_w_gate": nrm(ks[20], (DEPTH, MOE_EXPERTS, D_MODEL, MOE_D_FF), D_MODEL ** -0.5),
        "moe_w_up": nrm(ks[21], (DEPTH, MOE_EXPERTS, D_MODEL, MOE_D_FF), D_MODEL ** -0.5),
        "moe_w_down": nrm(ks[22], (DEPTH, MOE_EXPERTS, MOE_D_FF, D_MODEL), MOE_D_FF ** -0.5),
        "final_norm_w": 1.0 + nrm(ks[23], (D_MODEL,), 0.02),
    }


def reference(x, norm_mix_w, norm_ffn_w, ml_w_in, ml_b_i, ml_b_f, ml_norm_w, ml_w_out,
              ssd_w_in, ssd_conv_w, ssd_conv_b, ssd_dt_bias, ssd_a_log, ssd_d, ssd_norm_w, ssd_w_out,
              moe_w_group, moe_b_group, moe_w_expert, moe_b_expert, moe_w_gate, moe_w_up, moe_w_down,
              final_norm_w):
    h = x
    for layer in range(DEPTH):
        j = layer // N_MIXERS
        hn = rms_norm(h, norm_mix_w[layer])
        if layer % N_MIXERS == 0:
            mix = mlstm_mixer(hn, ml_w_in[j], ml_b_i[j], ml_b_f[j], ml_norm_w[j], ml_w_out[j])
        else:
            mix = ssd_mixer(hn, ssd_w_in[j], ssd_conv_w[j], ssd_conv_b[j], ssd_dt_bias[j],
                            ssd_a_log[j], ssd_d[j], ssd_norm_w[j], ssd_w_out[j])
        h = h + mix
        h = h + hier_moe(rms_norm(h, norm_ffn_w[layer]), moe_w_group[layer], moe_b_group[layer],
                         moe_w_expert[layer], moe_b_expert[layer], moe_w_gate[layer],
                         moe_w_up[layer], moe_w_down[layer])
    return rms_norm(h, final_norm_w)
```

```python
import functools

import jax
import jax.numpy as jnp
from jax import lax
from jax.experimental import pallas as pl
from jax.experimental.pallas import tpu as pltpu

F32 = jnp.float32
BF16 = jnp.bfloat16

LANES = 128
VMEM_LIMIT_BYTES = 48 * 1024 * 1024

NORM_EPS = 1e-6

ML_HEADS = 4
ML_CHUNK = 256

SSM_HEAD_DIM = 64
SSM_GROUPS = 8
SSM_STATE = 128
SSM_CONV = 4
SSM_CHUNK = 128
CONV_CARRY = 8

MOE_GROUPS = 8
MOE_EXPERTS_PER_GROUP = 8
MOE_EXPERTS = MOE_GROUPS * MOE_EXPERTS_PER_GROUP
MOE_TOP_K = 2
MOE_ROWS = 256


def _cparams(*semantics):
    return pltpu.CompilerParams(dimension_semantics=semantics, vmem_limit_bytes=VMEM_LIMIT_BYTES)


def _sigmoid(x):
    return 1.0 / (1.0 + jnp.exp(-x))


def _silu(x):
    return x * _sigmoid(x)


def _softplus(x):
    return jnp.maximum(x, 0.0) + jnp.log1p(jnp.exp(-jnp.abs(x)))


def _log_sigmoid(x):
    return jnp.minimum(x, 0.0) - jnp.log1p(jnp.exp(-jnp.abs(x)))


def _tri(n):
    r = lax.broadcasted_iota(jnp.int32, (n, n), 0)
    c = lax.broadcasted_iota(jnp.int32, (n, n), 1)
    causal = r >= c
    return causal.astype(F32), causal


def _route(logits):
    lane = lax.broadcasted_iota(jnp.int32, logits.shape, 1)
    neg = jnp.float32(-jnp.inf)
    big = jnp.int32(4 * LANES)
    gl = jnp.where(lane < MOE_GROUPS, logits, neg)
    gmax = jnp.max(gl, axis=1, keepdims=True)
    gsum = jnp.sum(jnp.exp(gl - gmax), axis=1, keepdims=True)
    g_w = 1.0 / gsum
    g_idx = jnp.min(jnp.where(gl == gmax, lane, big), axis=1, keepdims=True)
    lo = MOE_GROUPS + g_idx * MOE_EXPERTS_PER_GROUP
    el = jnp.where((lane >= lo) & (lane < lo + MOE_EXPERTS_PER_GROUP), logits, neg)
    emax = jnp.max(el, axis=1, keepdims=True)
    esum = jnp.sum(jnp.exp(el - emax), axis=1, keepdims=True)
    idx1 = jnp.min(jnp.where(el == emax, lane, big), axis=1, keepdims=True)
    el2 = jnp.where(lane == idx1, neg, el)
    emax2 = jnp.max(el2, axis=1, keepdims=True)
    idx2 = jnp.min(jnp.where(el2 == emax2, lane, big), axis=1, keepdims=True)
    p1 = 1.0 / esum
    p2 = jnp.exp(emax2 - emax) / esum
    psum = p1 + p2
    w1 = g_w * (p1 / psum)
    w2 = g_w * (p2 / psum)
    e1 = (idx1 - MOE_GROUPS).astype(F32)
    e2 = (idx2 - MOE_GROUPS).astype(F32)
    out = jnp.where(lane == 0, w1, jnp.where(lane == 1, w2, jnp.where(lane == 2, e1, jnp.where(lane == 3, e2, 0.0))))
    return out


def _norm_kernel(*refs, has_delta, post, emit_h):
    it = iter(refs)
    h_ref = next(it)
    d_ref = next(it) if has_delta else None
    nw_ref = next(it)
    ws_ref = next(it) if post else None
    b_ref = next(it) if post else None
    hnew_ref = next(it) if emit_h else None
    hn_ref = next(it)
    small_ref = next(it) if post else None

    h = h_ref[...]
    if has_delta:
        h = h + d_ref[...]
    if emit_h:
        hnew_ref[...] = h
    y = h * lax.rsqrt(jnp.mean(h * h, axis=-1, keepdims=True) + NORM_EPS) * nw_ref[...]
    hn = y.astype(hn_ref.dtype)
    hn_ref[...] = hn
    if post == "route":
        s = jnp.dot(y, ws_ref[...], preferred_element_type=F32, precision=lax.Precision.HIGHEST) + b_ref[...]
        small_ref[...] = _route(s)
    elif post:
        small_ref[...] = jnp.dot(hn, ws_ref[...], preferred_element_type=F32) + b_ref[...]


def _norm(h, delta, nw, ws=None, bias=None, *, post=None, emit_h=False, out_dtype=BF16, tm=256):
    T, D = h.shape
    has_delta = delta is not None
    row = pl.BlockSpec((tm, D), lambda i: (i, 0))
    in_specs, args = [row], [h]
    if has_delta:
        in_specs.append(row)
        args.append(delta)
    in_specs.append(pl.BlockSpec((1, D), lambda i: (0, 0)))
    args.append(nw.reshape(1, D).astype(F32))
    if post:
        in_specs += [pl.BlockSpec((D, LANES), lambda i: (0, 0)), pl.BlockSpec((1, LANES), lambda i: (0, 0))]
        args += [ws, bias]
    out_shape, out_specs = [], []
    if emit_h:
        out_shape.append(jax.ShapeDtypeStruct((T, D), F32))
        out_specs.append(row)
    out_shape.append(jax.ShapeDtypeStruct((T, D), out_dtype))
    out_specs.append(row)
    if post:
        out_shape.append(jax.ShapeDtypeStruct((T, LANES), F32))
        out_specs.append(pl.BlockSpec((tm, LANES), lambda i: (i, 0)))
    return pl.pallas_call(
        functools.partial(_norm_kernel, has_delta=has_delta, post=post, emit_h=emit_h),
        grid=(T // tm,),
        in_specs=in_specs,
        out_specs=out_specs,
        out_shape=out_shape,
        compiler_params=_cparams("parallel"),
        name="norm_" + (post or "plain"),
    )(*args)


def _pad_lanes(w):
    return jnp.pad(w, [(0, 0)] * (w.ndim - 1) + [(0, LANES - w.shape[-1])])


def _matmul_kernel(*refs, has_res):
    if has_res:
        a_ref, w_ref, r_ref, o_ref = refs
    else:
        a_ref, w_ref, o_ref = refs
    acc = jnp.dot(a_ref[...], w_ref[...], preferred_element_type=F32)
    if has_res:
        acc = acc + r_ref[...]
    o_ref[...] = acc.astype(o_ref.dtype)


def _matmul(a, w, res=None, *, out_dtype, tm=1024, tn=512):
    T, K = a.shape
    N = w.shape[1]
    in_specs = [pl.BlockSpec((tm, K), lambda i, j: (i, 0)), pl.BlockSpec((K, tn), lambda i, j: (0, j))]
    args = [a, w]
    if res is not None:
        in_specs.append(pl.BlockSpec((tm, tn), lambda i, j: (i, j)))
        args.append(res)
    return pl.pallas_call(
        functools.partial(_matmul_kernel, has_res=res is not None),
        grid=(T // tm, N // tn),
        in_specs=in_specs,
        out_specs=pl.BlockSpec((tm, tn), lambda i, j: (i, j)),
        out_shape=jax.ShapeDtypeStruct((T, N), out_dtype),
        compiler_params=_cparams("parallel", "parallel"),
        name="matmul_res" if res is not None else "matmul",
    )(*args)


def _mlstm_kernel(proj_ref, gates_ref, nw_ref, out_ref, c_ref, n_ref, m_ref, *, dk, dv):
    H = ML_HEADS
    L = proj_ref.shape[1]
    scale = dk ** -0.5

    @pl.when(pl.program_id(1) == 0)
    def _():
        c_ref[...] = jnp.zeros_like(c_ref)
        n_ref[...] = jnp.zeros_like(n_ref)
        m_ref[...] = jnp.zeros_like(m_ref)

    tri, causal = _tri(L)
    g = gates_ref[0]
    lf = _log_sigmoid(g)
    b = jnp.dot(tri, lf, preferred_element_type=F32, precision=lax.Precision.HIGHEST)
    g_t = g.T
    b_t = b.T
    neg = jnp.float32(-jnp.inf)

    for h in range(H):
        q = proj_ref[0, :, h * dk:(h + 1) * dk]
        k = proj_ref[0, :, H * dk + h * dk:H * dk + (h + 1) * dk]
        v = proj_ref[0, :, 2 * H * dk + h * dv:2 * H * dk + (h + 1) * dv]
        o = proj_ref[0, :, 2 * H * dk + H * dv + h * dv:2 * H * dk + H * dv + (h + 1) * dv]
        icol = g[:, h:h + 1]
        bcol = b[:, H + h:H + h + 1]
        irow = g_t[h:h + 1, :]
        brow = b_t[H + h:H + h + 1, :]
        m_prev = m_ref[h, 0:1, 0:1]
        c_prev = c_ref[h]
        n_prev = n_ref[h]

        logd = jnp.where(causal, bcol - brow + irow, neg)
        inter = bcol + m_prev
        m_t = jnp.maximum(inter, jnp.max(logd, axis=1, keepdims=True))
        s = lax.dot_general(q, k, (((1,), (1,)), ((), ())), preferred_element_type=F32)
        s = s * scale * jnp.exp(logd - m_t)
        sc = jnp.exp(inter - m_t)
        qc = jnp.dot(q, c_prev.astype(BF16), preferred_element_type=F32) * scale
        num = jnp.dot(s.astype(BF16), v, preferred_element_type=F32) + sc * qc
        qn = jnp.sum(q.astype(F32) * n_prev, axis=1, keepdims=True) * scale
        den = jnp.sum(s, axis=1, keepdims=True) + sc * qn
        hout = num / jnp.maximum(jnp.abs(den), jnp.exp(-m_t))
        hn = hout * lax.rsqrt(jnp.mean(hout * hout, axis=1, keepdims=True) + NORM_EPS)
        hn = hn * nw_ref[:, h * dv:(h + 1) * dv] * _sigmoid(o.astype(F32))
        out_ref[0, :, h * dv:(h + 1) * dv] = hn.astype(out_ref.dtype)
        b_last = bcol[L - 1:L, :]
        gcol = b_last - bcol + icol
        m_new = jnp.maximum(b_last + m_prev, jnp.max(gcol, axis=0, keepdims=True))
        wk = k.astype(F32) * jnp.exp(gcol - m_new)
        dec = jnp.exp(b_last + m_prev - m_new)
        c_ref[h] = dec * c_prev + lax.dot_general(
            wk.astype(BF16), v, (((0,), (0,)), ((), ())), preferred_element_type=F32)
        n_ref[h] = dec * n_prev + jnp.sum(wk, axis=0, keepdims=True)
        m_ref[h] = jnp.broadcast_to(m_new, m_ref.shape[1:])


def _mlstm(proj, gates, norm_w, *, dk, dv):
    B, S, W = proj.shape
    H = ML_HEADS
    L = ML_CHUNK
    return pl.pallas_call(
        functools.partial(_mlstm_kernel, dk=dk, dv=dv),
        grid=(B, S // L),
        in_specs=[
            pl.BlockSpec((1, L, W), lambda b, c: (b, c, 0)),
            pl.BlockSpec((1, L, LANES), lambda b, c: (b, c, 0)),
            pl.BlockSpec((1, H * dv), lambda b, c: (0, 0)),
        ],
        out_specs=pl.BlockSpec((1, L, H * dv), lambda b, c: (b, c, 0)),
        out_shape=jax.ShapeDtypeStruct((B, S, H * dv), BF16),
        scratch_shapes=[
            pltpu.VMEM((H, dk, dv), F32),
            pltpu.VMEM((H, 1, dk), F32),
            pltpu.VMEM((H, 8, LANES), F32),
        ],
        compiler_params=_cparams("parallel", "arbitrary"),
        name="mlstm",
    )(proj, gates, norm_w.reshape(1, H * dv).astype(F32))


def _ssd_kernel(proj_ref, dt_ref, cw_ref, cb_ref, alog_ref, dskip_ref, nw_ref, out_ref,
                ext_ref, u_ref, xw_ref, yg_ref, state_ref, *, d_inner):
    G, P, N = SSM_GROUPS, SSM_HEAD_DIM, SSM_STATE
    L = proj_ref.shape[1]
    conv_dim = d_inner + 2 * G * N
    hpg = d_inner // (G * P)
    gw = hpg * P

    @pl.when(pl.program_id(1) == 0)
    def _():
        ext_ref[0:CONV_CARRY, :] = jnp.zeros((CONV_CARRY, conv_dim), F32)
        state_ref[...] = jnp.zeros_like(state_ref)

    ext_ref[CONV_CARRY:CONV_CARRY + L, :] = proj_ref[0, :, d_inner:].astype(F32)
    slab = 512
    for c0 in range(0, conv_dim, slab):
        cs = slice(c0, c0 + slab)
        acc = cb_ref[:, cs] + cw_ref[SSM_CONV - 1:SSM_CONV, cs] * ext_ref[CONV_CARRY:CONV_CARRY + L, cs]
        for kk in range(SSM_CONV - 1):
            sh = SSM_CONV - 1 - kk
            acc = acc + cw_ref[kk:kk + 1, cs] * ext_ref[CONV_CARRY - sh:CONV_CARRY - sh + L, cs]
        u_ref[:, cs] = _silu(acc).astype(u_ref.dtype)
    ext_ref[0:CONV_CARRY, :] = ext_ref[L:L + CONV_CARRY, :]

    tri, causal = _tri(L)
    dt = _softplus(dt_ref[0])
    a_row = -jnp.exp(alog_ref[...])
    cum = jnp.dot(tri, dt * a_row, preferred_element_type=F32, precision=lax.Precision.HIGHEST)
    cum_last = cum[L - 1:L, :]
    expcum = jnp.exp(cum)
    to_end = jnp.exp(cum_last - cum) * dt
    dec_last = jnp.exp(cum_last)
    cum_t = cum.T
    dt_t = dt.T
    neg = jnp.float32(-jnp.inf)

    for g in range(G):
        bm = u_ref[:, d_inner + g * N:d_inner + (g + 1) * N]
        cm = u_ref[:, d_inner + G * N + g * N:d_inner + G * N + (g + 1) * N]
        st = state_ref[g]
        cb = lax.dot_general(cm, bm, (((1,), (1,)), ((), ())), preferred_element_type=F32)
        yoff = lax.dot_general(cm, st.astype(BF16), (((1,), (1,)), ((), ())), preferred_element_type=F32)
        for j in range(hpg):
            hd = g * hpg + j
            xh = u_ref[:, hd * P:(hd + 1) * P]
            xf = xh.astype(F32)
            seg = cum[:, hd:hd + 1] - cum_t[hd:hd + 1, :]
            wts = cb * jnp.exp(jnp.where(causal, seg, neg)) * dt_t[hd:hd + 1, :]
            y = jnp.dot(wts.astype(BF16), xh, preferred_element_type=F32)
            y = y + yoff[:, j * P:(j + 1) * P] * expcum[:, hd:hd + 1]
            y = y + dskip_ref[:, hd * P:(hd + 1) * P] * xf
            yg_ref[:, j * P:(j + 1) * P] = y
            xw_ref[:, j * P:(j + 1) * P] = (xf * to_end[:, hd:hd + 1]).astype(xw_ref.dtype)
            state_ref[g, j * P:(j + 1) * P, :] = st[j * P:(j + 1) * P, :] * dec_last[:, hd:hd + 1]
        state_ref[g] = state_ref[g] + lax.dot_general(
            xw_ref[...], bm, (((0,), (0,)), ((), ())), preferred_element_type=F32)
        z = proj_ref[0, :, g * gw:(g + 1) * gw].astype(F32)
        yg = yg_ref[...] * _silu(z)
        yg = yg * lax.rsqrt(jnp.mean(yg * yg, axis=1, keepdims=True) + NORM_EPS) * nw_ref[:, g * gw:(g + 1) * gw]
        out_ref[0, :, g * gw:(g + 1) * gw] = yg.astype(out_ref.dtype)


def _ssd(proj, dt, conv_w, conv_b, a_log, d_skip, norm_w, *, d_inner):
    B, S, W = proj.shape
    G, P, N = SSM_GROUPS, SSM_HEAD_DIM, SSM_STATE
    L = SSM_CHUNK
    conv_dim = W - d_inner
    gw = d_inner // G
    const = lambda b, c: (0, 0)
    return pl.pallas_call(
        functools.partial(_ssd_kernel, d_inner=d_inner),
        grid=(B, S // L),
        in_specs=[
            pl.BlockSpec((1, L, W), lambda b, c: (b, c, 0)),
            pl.BlockSpec((1, L, LANES), lambda b, c: (b, c, 0)),
            pl.BlockSpec((SSM_CONV, conv_dim), const),
            pl.BlockSpec((1, conv_dim), const),
            pl.BlockSpec((1, LANES), const),
            pl.BlockSpec((1, d_inner), const),
            pl.BlockSpec((1, d_inner), const),
        ],
        out_specs=pl.BlockSpec((1, L, d_inner), lambda b, c: (b, c, 0)),
        out_shape=jax.ShapeDtypeStruct((B, S, d_inner), BF16),
        scratch_shapes=[
            pltpu.VMEM((L + CONV_CARRY, conv_dim), F32),
            pltpu.VMEM((L, conv_dim), BF16),
            pltpu.VMEM((L, gw), BF16),
            pltpu.VMEM((L, gw), F32),
            pltpu.VMEM((G, gw, N), F32),
        ],
        compiler_params=_cparams("parallel", "arbitrary"),
        name="ssd",
    )(proj, dt, conv_w.astype(F32), conv_b.reshape(1, conv_dim).astype(F32),
      _pad_lanes(a_log.reshape(1, -1).astype(F32)),
      jnp.repeat(d_skip.astype(F32), P).reshape(1, d_inner), norm_w.reshape(1, d_inner).astype(F32))


def _expert_kernel(blk_e_ref, n_used_ref, x_ref, rw_ref, wg_ref, wu_ref, wd_ref, o_ref):
    i = pl.program_id(0)

    @pl.when(i < n_used_ref[0])
    def _():
        x = x_ref[...]
        hg = jnp.dot(x, wg_ref[0], preferred_element_type=F32)
        hu = jnp.dot(x, wu_ref[0], preferred_element_type=F32)
        hid = (_silu(hg) * hu).astype(BF16)
        y = jnp.dot(hid, wd_ref[0], preferred_element_type=F32)
        o_ref[...] = (y * rw_ref[...]).astype(o_ref.dtype)

    @pl.when(i >= n_used_ref[0])
    def _():
        o_ref[...] = jnp.zeros_like(o_ref)


def _experts(xb, rows_w, blk_e, n_used, w_gate, w_up, w_down):
    n_rows, D = xb.shape
    F = w_gate.shape[-1]
    R = MOE_ROWS
    return pl.pallas_call(
        _expert_kernel,
        grid_spec=pltpu.PrefetchScalarGridSpec(
            num_scalar_prefetch=2,
            grid=(n_rows // R,),
            in_specs=[
                pl.BlockSpec((R, D), lambda i, be, nu: (i, 0)),
                pl.BlockSpec((R, 1), lambda i, be, nu: (i, 0)),
                pl.BlockSpec((1, D, F), lambda i, be, nu: (be[i], 0, 0)),
                pl.BlockSpec((1, D, F), lambda i, be, nu: (be[i], 0, 0)),
                pl.BlockSpec((1, F, D), lambda i, be, nu: (be[i], 0, 0)),
            ],
            out_specs=pl.BlockSpec((R, D), lambda i, be, nu: (i, 0)),
        ),
        out_shape=jax.ShapeDtypeStruct((n_rows, D), F32),
        compiler_params=_cparams("arbitrary"),
        name="experts",
    )(blk_e, n_used, xb, rows_w, w_gate, w_up, w_down)


def _moe(xn, route, w_gate, w_up, w_down):
    T, D = xn.shape
    A = T * MOE_TOP_K
    R = MOE_ROWS
    flat_w = route[:, 0:MOE_TOP_K].reshape(A)
    flat_e = route[:, MOE_TOP_K:2 * MOE_TOP_K].astype(jnp.int32).reshape(A)
    flat_tok = jnp.repeat(jnp.arange(T, dtype=jnp.int32), MOE_TOP_K)
    order = jnp.argsort(flat_e)
    se, stok, sw = flat_e[order], flat_tok[order], flat_w[order]
    counts = jnp.bincount(flat_e, length=MOE_EXPERTS)
    starts = jnp.cumsum(counts) - counts
    padded = (counts + R - 1) // R * R
    pends = jnp.cumsum(padded)
    pstarts = pends - padded
    dest = (pstarts[se] + jnp.arange(A, dtype=jnp.int32) - starts[se]).astype(jnp.int32)
    n_rows = A + MOE_EXPERTS * R
    n_blocks = n_rows // R
    rows_tok = jnp.full((n_rows,), T, jnp.int32).at[dest].set(stok)
    rows_w = jnp.zeros((n_rows,), F32).at[dest].set(sw)
    blk_e = jnp.minimum(jnp.searchsorted(pends, jnp.arange(n_blocks) * R, side='right'),
                        MOE_EXPERTS - 1).astype(jnp.int32)
    n_used = (pends[-1] // R).astype(jnp.int32).reshape(1)
    x_pad = jnp.concatenate([xn, jnp.zeros((1, D), xn.dtype)], axis=0)
    xb = x_pad[rows_tok]
    yb = _experts(xb, rows_w.reshape(n_rows, 1), blk_e, n_used, w_gate, w_up, w_down)
    return jax.ops.segment_sum(yb, rows_tok, num_segments=T + 1)[:T]


def kernel(x, norm_mix_w, norm_ffn_w, ml_w_in, ml_b_i, ml_b_f, ml_norm_w, ml_w_out, ssd_w_in, ssd_conv_w, ssd_conv_b, ssd_dt_bias, ssd_a_log, ssd_d, ssd_norm_w, ssd_w_out, moe_w_group, moe_b_group, moe_w_expert, moe_b_expert, moe_w_gate, moe_w_up, moe_w_down, final_norm_w):
    B, S, D = x.shape
    T = B * S
    depth = norm_mix_w.shape[0]
    h = x.reshape(T, D)
    delta = None
    for layer in range(depth):
        j = layer // 2
        if layer % 2 == 0:
            H = ML_HEADS
            dv = ml_w_out.shape[1] // H
            dk = dv // 2
            main = 2 * H * dk + 2 * H * dv
            w_in = ml_w_in[j]
            ws = _pad_lanes(w_in[:, main:]).astype(BF16)
            bias = _pad_lanes(jnp.concatenate([ml_b_i[j], ml_b_f[j]]).reshape(1, 2 * H).astype(F32))
        else:
            d_inner = ssd_w_out.shape[1]
            main = d_inner + ssd_conv_w.shape[2]
            w_in = ssd_w_in[j]
            ws = _pad_lanes(w_in[:, main:]).astype(BF16)
            bias = _pad_lanes(ssd_dt_bias[j].reshape(1, -1).astype(F32))
        if delta is None:
            hn, small = _norm(h, None, norm_mix_w[layer], ws, bias, post="plain")
        else:
            h, hn, small = _norm(h, delta, norm_mix_w[layer], ws, bias, post="plain", emit_h=True)
        proj = _matmul(hn, w_in[:, :main].astype(BF16), out_dtype=BF16)
        if layer % 2 == 0:
            y = _mlstm(proj.reshape(B, S, main), small.reshape(B, S, LANES), ml_norm_w[j], dk=dk, dv=dv)
            w_out = ml_w_out[j]
        else:
            y = _ssd(proj.reshape(B, S, main), small.reshape(B, S, LANES), ssd_conv_w[j], ssd_conv_b[j],
                     ssd_a_log[j], ssd_d[j], ssd_norm_w[j], d_inner=d_inner)
            w_out = ssd_w_out[j]
        h = _matmul(y.reshape(T, -1), w_out.astype(BF16), h, out_dtype=F32)
        wr = _pad_lanes(jnp.concatenate([moe_w_group[layer], moe_w_expert[layer]], axis=1)).astype(F32)
        br = _pad_lanes(jnp.concatenate([moe_b_group[layer], moe_b_expert[layer]]).reshape(1, -1).astype(F32))
        xn, route = _norm(h, None, norm_ffn_w[layer], wr, br, post="route")
        delta = _moe(xn, route, moe_w_gate[layer].astype(BF16), moe_w_up[layer].astype(BF16),
                     moe_w_down[layer].astype(BF16))
    out = _norm(h, delta, final_norm_w, out_dtype=F32)[0]
    return out.reshape(B, S, D)
```

```python
import functools

import jax
import jax.numpy as jnp
from jax import lax
from jax.experimental import pallas as pl
from jax.experimental.pallas import tpu as pltpu

F32 = jnp.float32
BF16 = jnp.bfloat16

LANES = 128
VMEM_LIMIT_BYTES = 48 * 1024 * 1024
EXPERT_VMEM_LIMIT_BYTES = 56 * 1024 * 1024

NORM_EPS = 1e-6

ML_HEADS = 4
ML_CHUNK = 256

SSM_HEAD_DIM = 64
SSM_GROUPS = 8
SSM_STATE = 128
SSM_CONV = 4
SSM_CHUNK = 128
CONV_CARRY = 8

MOE_GROUPS = 8
MOE_EXPERTS_PER_GROUP = 8
MOE_EXPERTS = MOE_GROUPS * MOE_EXPERTS_PER_GROUP
MOE_TOP_K = 2
MOE_ROWS = 256


def _cparams(*semantics):
    return pltpu.CompilerParams(dimension_semantics=semantics, vmem_limit_bytes=VMEM_LIMIT_BYTES)


def _sigmoid(x):
    return 1.0 / (1.0 + jnp.exp(-x))


def _silu(x):
    return x * _sigmoid(x)


def _softplus(x):
    return jnp.maximum(x, 0.0) + jnp.log1p(jnp.exp(-jnp.abs(x)))


def _log_sigmoid(x):
    return jnp.minimum(x, 0.0) - jnp.log1p(jnp.exp(-jnp.abs(x)))


def _tri(n):
    r = lax.broadcasted_iota(jnp.int32, (n, n), 0)
    c = lax.broadcasted_iota(jnp.int32, (n, n), 1)
    causal = r >= c
    return causal.astype(F32), causal


PACK_COLS = 2 * LANES
HI_MASK = 0xFFFF0000


def _pack_store(ref, y, rows):
    nchunk = y.shape[1] // PACK_COLS
    for s in range(nchunk):
        hi = y[:, s * PACK_COLS:s * PACK_COLS + LANES].astype(BF16).astype(F32)
        lo = y[:, s * PACK_COLS + LANES:(s + 1) * PACK_COLS].astype(BF16).astype(F32)
        word = lax.bitcast_convert_type(hi, jnp.uint32) | (lax.bitcast_convert_type(lo, jnp.uint32) >> 16)
        ref[pl.ds(s, rows, stride=nchunk), :] = word


def _unpack_f32(ref, s, rows, nchunk):
    w = ref[pl.ds(s, rows, stride=nchunk), :]
    hi = lax.bitcast_convert_type(w & jnp.uint32(HI_MASK), F32)
    lo = lax.bitcast_convert_type(w << 16, F32)
    return hi, lo


def _row_copy(src_ref, src_row, dst_ref, dst_row, nchunk, sem):
    return pltpu.make_async_copy(
        src_ref.at[pl.ds(pl.multiple_of(src_row * nchunk, nchunk), nchunk), :],
        dst_ref.at[pl.ds(pl.multiple_of(dst_row * nchunk, nchunk), nchunk), :],
        sem)


def _route(logits):
    lane = lax.broadcasted_iota(jnp.int32, logits.shape, 1)
    neg = jnp.float32(-jnp.inf)
    big = jnp.int32(4 * LANES)
    gl = jnp.where(lane < MOE_GROUPS, logits, neg)
    gmax = jnp.max(gl, axis=1, keepdims=True)
    gsum = jnp.sum(jnp.exp(gl - gmax), axis=1, keepdims=True)
    g_w = 1.0 / gsum
    g_idx = jnp.min(jnp.where(gl == gmax, lane, big), axis=1, keepdims=True)
    lo = MOE_GROUPS + g_idx * MOE_EXPERTS_PER_GROUP
    el = jnp.where((lane >= lo) & (lane < lo + MOE_EXPERTS_PER_GROUP), logits, neg)
    emax = jnp.max(el, axis=1, keepdims=True)
    esum = jnp.sum(jnp.exp(el - emax), axis=1, keepdims=True)
    idx1 = jnp.min(jnp.where(el == emax, lane, big), axis=1, keepdims=True)
    el2 = jnp.where(lane == idx1, neg, el)
    emax2 = jnp.max(el2, axis=1, keepdims=True)
    idx2 = jnp.min(jnp.where(el2 == emax2, lane, big), axis=1, keepdims=True)
    p1 = 1.0 / esum
    p2 = jnp.exp(emax2 - emax) / esum
    psum = p1 + p2
    w1 = g_w * (p1 / psum)
    w2 = g_w * (p2 / psum)
    e1 = (idx1 - MOE_GROUPS).astype(F32)
    e2 = (idx2 - MOE_GROUPS).astype(F32)
    out = jnp.where(lane == 0, w1, jnp.where(lane == 1, w2, jnp.where(lane == 2, e1, jnp.where(lane == 3, e2, 0.0))))
    return out


def _norm_kernel(*refs, combine, post, emit_h, tm):
    it = iter(refs)
    if combine:
        dest_ref, gate_ref, yb_ref = next(it), next(it), next(it)
    h_ref, nw_ref = next(it), next(it)
    ws_ref = next(it) if post else None
    b_ref = next(it) if post else None
    hnew_ref = next(it) if emit_h else None
    hn_ref = next(it)
    small_ref = next(it) if post else None
    counts_ref = next(it) if post == "route" else None
    if combine:
        y0_ref, y1_ref, sem = next(it), next(it), next(it)

    D = h_ref.shape[1]
    nchunk = D // PACK_COLS
    h = h_ref[...]
    if combine:
        def issue(r, c):
            _row_copy(yb_ref, dest_ref[0, 0, r], y0_ref, r, nchunk, sem).start()
            _row_copy(yb_ref, dest_ref[0, 1, r], y1_ref, r, nchunk, sem).start()
            return c

        def drain(r, c):
            _row_copy(yb_ref, 0, y0_ref, r, nchunk, sem).wait()
            _row_copy(yb_ref, 0, y1_ref, r, nchunk, sem).wait()
            return c

        lax.fori_loop(0, tm, issue, 0, unroll=8)
        lax.fori_loop(0, tm, drain, 0, unroll=8)
        w0 = gate_ref[:, 0:1]
        w1 = gate_ref[:, 1:2]
        pieces = []
        for s in range(nchunk):
            a_hi, a_lo = _unpack_f32(y0_ref, s, tm, nchunk)
            b_hi, b_lo = _unpack_f32(y1_ref, s, tm, nchunk)
            pieces.append(h[:, s * PACK_COLS:s * PACK_COLS + LANES] + (w0 * a_hi + w1 * b_hi))
            pieces.append(h[:, s * PACK_COLS + LANES:(s + 1) * PACK_COLS] + (w0 * a_lo + w1 * b_lo))
        h = jnp.concatenate(pieces, axis=1)
    if emit_h:
        hnew_ref[...] = h
    y = h * lax.rsqrt(jnp.mean(h * h, axis=-1, keepdims=True) + NORM_EPS) * nw_ref[...]
    if post == "route":
        _pack_store(hn_ref, y, tm)
        s = jnp.dot(y, ws_ref[...], preferred_element_type=F32, precision=lax.Precision.HIGHEST) + b_ref[...]
        r = _route(s)
        small_ref[...] = r
        lane = lax.broadcasted_iota(jnp.int32, r.shape, 1).astype(F32)
        hits = (lane == r[:, 2:3]).astype(F32) + (lane == r[:, 3:4]).astype(F32)

        @pl.when(pl.program_id(0) == 0)
        def _():
            counts_ref[...] = jnp.zeros_like(counts_ref)

        counts_ref[...] += jnp.broadcast_to(jnp.sum(hits, axis=0, keepdims=True), counts_ref.shape)
    else:
        hn = y.astype(hn_ref.dtype)
        hn_ref[...] = hn
        if post:
            small_ref[...] = jnp.dot(hn, ws_ref[...], preferred_element_type=F32) + b_ref[...]


def _norm(h, nw, ws=None, bias=None, *, combine=None, post=None, emit_h=False, out_dtype=BF16, tm=256):
    T, D = h.shape
    nchunk = D // PACK_COLS
    row = pl.BlockSpec((tm, D), lambda i: (i, 0))
    in_specs, args, scratch = [], [], []
    if combine is not None:
        yb, dest, gate = combine
        in_specs += [pl.BlockSpec((1, 8, tm), lambda i: (i, 0, 0), memory_space=pltpu.SMEM),
                     pl.BlockSpec((tm, LANES), lambda i: (i, 0)),
                     pl.BlockSpec(memory_space=pl.ANY)]
        args += [dest, gate, yb]
        scratch = [pltpu.VMEM((tm * nchunk, LANES), jnp.uint32), pltpu.VMEM((tm * nchunk, LANES), jnp.uint32),
                   pltpu.SemaphoreType.DMA]
    in_specs += [row, pl.BlockSpec((1, D), lambda i: (0, 0))]
    args += [h, nw.reshape(1, D).astype(F32)]
    if post:
        in_specs += [pl.BlockSpec((D, LANES), lambda i: (0, 0)), pl.BlockSpec((1, LANES), lambda i: (0, 0))]
        args += [ws, bias]
    out_shape, out_specs = [], []
    if emit_h:
        out_shape.append(jax.ShapeDtypeStruct((T, D), F32))
        out_specs.append(row)
    if post == "route":
        out_shape.append(jax.ShapeDtypeStruct((T * nchunk, LANES), jnp.uint32))
        out_specs.append(pl.BlockSpec((tm * nchunk, LANES), lambda i: (i, 0)))
    else:
        out_shape.append(jax.ShapeDtypeStruct((T, D), out_dtype))
        out_specs.append(row)
    if post:
        out_shape.append(jax.ShapeDtypeStruct((T, LANES), F32))
        out_specs.append(pl.BlockSpec((tm, LANES), lambda i: (i, 0)))
    if post == "route":
        out_shape.append(jax.ShapeDtypeStruct((8, LANES), F32))
        out_specs.append(pl.BlockSpec((8, LANES), lambda i: (0, 0)))
    return pl.pallas_call(
        functools.partial(_norm_kernel, combine=combine is not None, post=post, emit_h=emit_h, tm=tm),
        grid=(T // tm,),
        in_specs=in_specs,
        out_specs=out_specs,
        out_shape=out_shape,
        scratch_shapes=scratch,
        compiler_params=_cparams("arbitrary"),
        name="norm_" + (post or "plain") + ("_combine" if combine is not None else ""),
    )(*args)


def _pad_lanes(w):
    return jnp.pad(w, [(0, 0)] * (w.ndim - 1) + [(0, LANES - w.shape[-1])])


def _matmul_kernel(*refs, has_res):
    if has_res:
        a_ref, w_ref, r_ref, o_ref = refs
    else:
        a_ref, w_ref, o_ref = refs
    acc = jnp.dot(a_ref[...], w_ref[...], preferred_element_type=F32)
    if has_res:
        acc = acc + r_ref[...]
    o_ref[...] = acc.astype(o_ref.dtype)


def _matmul(a, w, res=None, *, n_cols=None, out_dtype, tm=1024, tn=512):
    T, K = a.shape
    N = n_cols or w.shape[1]
    in_specs = [pl.BlockSpec((tm, K), lambda i, j: (i, 0)), pl.BlockSpec((K, tn), lambda i, j: (0, j))]
    args = [a, w]
    if res is not None:
        in_specs.append(pl.BlockSpec((tm, tn), lambda i, j: (i, j)))
        args.append(res)
    return pl.pallas_call(
        functools.partial(_matmul_kernel, has_res=res is not None),
        grid=(T // tm, N // tn),
        in_specs=in_specs,
        out_specs=pl.BlockSpec((tm, tn), lambda i, j: (i, j)),
        out_shape=jax.ShapeDtypeStruct((T, N), out_dtype),
        compiler_params=_cparams("parallel", "parallel"),
        name="matmul_res" if res is not None else "matmul",
    )(*args)


def _mlstm_kernel(proj_ref, gates_ref, nw_ref, out_ref, c_ref, n_ref, m_ref, *, dk, dv):
    H = ML_HEADS
    L = proj_ref.shape[1]
    scale = dk ** -0.5

    @pl.when(pl.program_id(1) == 0)
    def _():
        c_ref[...] = jnp.zeros_like(c_ref)
        n_ref[...] = jnp.zeros_like(n_ref)
        m_ref[...] = jnp.zeros_like(m_ref)

    tri, causal = _tri(L)
    g = gates_ref[0]
    lf = _log_sigmoid(g)
    b = jnp.dot(tri, lf, preferred_element_type=F32, precision=lax.Precision.HIGHEST)
    g_t = g.T
    b_t = b.T
    neg = jnp.float32(-jnp.inf)

    for h in range(H):
        q = proj_ref[0, :, h * dk:(h + 1) * dk]
        k = proj_ref[0, :, H * dk + h * dk:H * dk + (h + 1) * dk]
        v = proj_ref[0, :, 2 * H * dk + h * dv:2 * H * dk + (h + 1) * dv]
        o = proj_ref[0, :, 2 * H * dk + H * dv + h * dv:2 * H * dk + H * dv + (h + 1) * dv]
        icol = g[:, h:h + 1]
        bcol = b[:, H + h:H + h + 1]
        irow = g_t[h:h + 1, :]
        brow = b_t[H + h:H + h + 1, :]
        m_prev = m_ref[h, 0:1, 0:1]
        c_prev = c_ref[h]
        n_prev = n_ref[h]

        logd = jnp.where(causal, bcol - brow + irow, neg)
        inter = bcol + m_prev
        m_t = jnp.maximum(inter, jnp.max(logd, axis=1, keepdims=True))
        s = lax.dot_general(q, k, (((1,), (1,)), ((), ())), preferred_element_type=F32)
        s = s * scale * jnp.exp(logd - m_t)
        sc = jnp.exp(inter - m_t)
        qc = jnp.dot(q, c_prev.astype(BF16), preferred_element_type=F32) * scale
        num = jnp.dot(s.astype(BF16), v, preferred_element_type=F32) + sc * qc
        qn = jnp.sum(q.astype(F32) * n_prev, axis=1, keepdims=True) * scale
        den = jnp.sum(s, axis=1, keepdims=True) + sc * qn
        hout = num / jnp.maximum(jnp.abs(den), jnp.exp(-m_t))
        hn = hout * lax.rsqrt(jnp.mean(hout * hout, axis=1, keepdims=True) + NORM_EPS)
        hn = hn * nw_ref[:, h * dv:(h + 1) * dv] * _sigmoid(o.astype(F32))
        out_ref[0, :, h * dv:(h + 1) * dv] = hn.astype(out_ref.dtype)
        b_last = bcol[L - 1:L, :]
        gcol = b_last - bcol + icol
        m_new = jnp.maximum(b_last + m_prev, jnp.max(gcol, axis=0, keepdims=True))
        wk = k.astype(F32) * jnp.exp(gcol - m_new)
        dec = jnp.exp(b_last + m_prev - m_new)
        c_ref[h] = dec * c_prev + lax.dot_general(
            wk.astype(BF16), v, (((0,), (0,)), ((), ())), preferred_element_type=F32)
        n_ref[h] = dec * n_prev + jnp.sum(wk, axis=0, keepdims=True)
        m_ref[h] = jnp.broadcast_to(m_new, m_ref.shape[1:])


def _mlstm(proj, gates, norm_w, *, dk, dv):
    B, S, W = proj.shape
    H = ML_HEADS
    L = ML_CHUNK
    return pl.pallas_call(
        functools.partial(_mlstm_kernel, dk=dk, dv=dv),
        grid=(B, S // L),
        in_specs=[
            pl.BlockSpec((1, L, W), lambda b, c: (b, c, 0)),
            pl.BlockSpec((1, L, LANES), lambda b, c: (b, c, 0)),
            pl.BlockSpec((1, H * dv), lambda b, c: (0, 0)),
        ],
        out_specs=pl.BlockSpec((1, L, H * dv), lambda b, c: (b, c, 0)),
        out_shape=jax.ShapeDtypeStruct((B, S, H * dv), BF16),
        scratch_shapes=[
            pltpu.VMEM((H, dk, dv), F32),
            pltpu.VMEM((H, 1, dk), F32),
            pltpu.VMEM((H, 8, LANES), F32),
        ],
        compiler_params=_cparams("parallel", "arbitrary"),
        name="mlstm",
    )(proj, gates, norm_w.reshape(1, H * dv).astype(F32))


def _ssd_kernel(proj_ref, dt_ref, cw_ref, cb_ref, alog_ref, dskip_ref, nw_ref, out_ref,
                ext_ref, u_ref, xw_ref, yg_ref, state_ref, *, d_inner):
    G, P, N = SSM_GROUPS, SSM_HEAD_DIM, SSM_STATE
    L = proj_ref.shape[1]
    conv_dim = d_inner + 2 * G * N
    hpg = d_inner // (G * P)
    gw = hpg * P

    @pl.when(pl.program_id(1) == 0)
    def _():
        ext_ref[0:CONV_CARRY, :] = jnp.zeros((CONV_CARRY, conv_dim), F32)
        state_ref[...] = jnp.zeros_like(state_ref)

    ext_ref[CONV_CARRY:CONV_CARRY + L, :] = proj_ref[0, :, d_inner:].astype(F32)
    slab = 512
    for c0 in range(0, conv_dim, slab):
        cs = slice(c0, c0 + slab)
        acc = cb_ref[:, cs] + cw_ref[SSM_CONV - 1:SSM_CONV, cs] * ext_ref[CONV_CARRY:CONV_CARRY + L, cs]
        for kk in range(SSM_CONV - 1):
            sh = SSM_CONV - 1 - kk
            acc = acc + cw_ref[kk:kk + 1, cs] * ext_ref[CONV_CARRY - sh:CONV_CARRY - sh + L, cs]
        u_ref[:, cs] = _silu(acc).astype(u_ref.dtype)
    ext_ref[0:CONV_CARRY, :] = ext_ref[L:L + CONV_CARRY, :]

    tri, causal = _tri(L)
    dt = _softplus(dt_ref[0])
    a_row = -jnp.exp(alog_ref[...])
    cum = jnp.dot(tri, dt * a_row, preferred_element_type=F32, precision=lax.Precision.HIGHEST)
    cum_last = cum[L - 1:L, :]
    expcum = jnp.exp(cum)
    to_end = jnp.exp(cum_last - cum) * dt
    dec_last = jnp.exp(cum_last)
    cum_t = cum.T
    dt_t = dt.T
    neg = jnp.float32(-jnp.inf)

    for g in range(G):
        bm = u_ref[:, d_inner + g * N:d_inner + (g + 1) * N]
        cm = u_ref[:, d_inner + G * N + g * N:d_inner + G * N + (g + 1) * N]
        st = state_ref[g]
        cb = lax.dot_general(cm, bm, (((1,), (1,)), ((), ())), preferred_element_type=F32)
        yoff = lax.dot_general(cm, st.astype(BF16), (((1,), (1,)), ((), ())), preferred_element_type=F32)
        for j in range(hpg):
            hd = g * hpg + j
            xh = u_ref[:, hd * P:(hd + 1) * P]
            xf = xh.astype(F32)
            seg = cum[:, hd:hd + 1] - cum_t[hd:hd + 1, :]
            wts = cb * jnp.exp(jnp.where(causal, seg, neg)) * dt_t[hd:hd + 1, :]
            y = jnp.dot(wts.astype(BF16), xh, preferred_element_type=F32)
            y = y + yoff[:, j * P:(j + 1) * P] * expcum[:, hd:hd + 1]
            y = y + dskip_ref[:, hd * P:(hd + 1) * P] * xf
            yg_ref[:, j * P:(j + 1) * P] = y
            xw_ref[:, j * P:(j + 1) * P] = (xf * to_end[:, hd:hd + 1]).astype(xw_ref.dtype)
            state_ref[g, j * P:(j + 1) * P, :] = st[j * P:(j + 1) * P, :] * dec_last[:, hd:hd + 1]
        state_ref[g] = state_ref[g] + lax.dot_general(
            xw_ref[...], bm, (((0,), (0,)), ((), ())), preferred_element_type=F32)
        z = proj_ref[0, :, g * gw:(g + 1) * gw].astype(F32)
        yg = yg_ref[...] * _silu(z)
        yg = yg * lax.rsqrt(jnp.mean(yg * yg, axis=1, keepdims=True) + NORM_EPS) * nw_ref[:, g * gw:(g + 1) * gw]
        out_ref[0, :, g * gw:(g + 1) * gw] = yg.astype(out_ref.dtype)


def _ssd(proj, dt, conv_w, conv_b, a_log, d_skip, norm_w, *, d_inner):
    B, S, W = proj.shape
    G, P, N = SSM_GROUPS, SSM_HEAD_DIM, SSM_STATE
    L = SSM_CHUNK
    conv_dim = W - d_inner
    gw = d_inner // G
    const = lambda b, c: (0, 0)
    return pl.pallas_call(
        functools.partial(_ssd_kernel, d_inner=d_inner),
        grid=(B, S // L),
        in_specs=[
            pl.BlockSpec((1, L, W), lambda b, c: (b, c, 0)),
            pl.BlockSpec((1, L, LANES), lambda b, c: (b, c, 0)),
            pl.BlockSpec((SSM_CONV, conv_dim), const),
            pl.BlockSpec((1, conv_dim), const),
            pl.BlockSpec((1, LANES), const),
            pl.BlockSpec((1, d_inner), const),
            pl.BlockSpec((1, d_inner), const),
        ],
        out_specs=pl.BlockSpec((1, L, d_inner), lambda b, c: (b, c, 0)),
        out_shape=jax.ShapeDtypeStruct((B, S, d_inner), BF16),
        scratch_shapes=[
            pltpu.VMEM((L + CONV_CARRY, conv_dim), F32),
            pltpu.VMEM((L, conv_dim), BF16),
            pltpu.VMEM((L, gw), BF16),
            pltpu.VMEM((L, gw), F32),
            pltpu.VMEM((G, gw, N), F32),
        ],
        compiler_params=_cparams("parallel", "arbitrary"),
        name="ssd",
    )(proj, dt, conv_w.astype(F32), conv_b.reshape(1, conv_dim).astype(F32),
      _pad_lanes(a_log.reshape(1, -1).astype(F32)),
      jnp.repeat(d_skip.astype(F32), P).reshape(1, d_inner), norm_w.reshape(1, d_inner).astype(F32))


def _plan_kernel(route_ref, counts_ref, dest_ref, run_ref, *, tm):
    @pl.when(pl.program_id(0) == 0)
    def _():
        lane8 = lax.broadcasted_iota(jnp.int32, (8, LANES), 1)
        padded = jnp.ceil(counts_ref[...] * (1.0 / MOE_ROWS)) * MOE_ROWS
        incl = padded
        sh = 1
        while sh < MOE_EXPERTS:
            incl = incl + jnp.where(lane8 >= sh, pltpu.roll(incl, sh, 1), 0.0)
            sh *= 2
        run_ref[...] = incl - padded

    r = route_ref[...]
    lane = lax.broadcasted_iota(jnp.int32, r.shape, 1)
    lanef = lane.astype(F32)
    oh0 = lanef == r[:, MOE_TOP_K:MOE_TOP_K + 1]
    oh1 = lanef == r[:, MOE_TOP_K + 1:MOE_TOP_K + 2]
    rr = lax.broadcasted_iota(jnp.int32, (tm, tm), 0)
    cc = lax.broadcasted_iota(jnp.int32, (tm, tm), 1)
    before = (rr > cc).astype(BF16)
    run = run_ref[0:1, :]
    p0 = jnp.dot(before, oh0.astype(BF16), preferred_element_type=F32)
    d0 = jnp.sum(jnp.where(oh0, run + p0, 0.0), axis=1, keepdims=True)
    run = run + jnp.sum(oh0.astype(F32), axis=0, keepdims=True)
    p1 = jnp.dot(before, oh1.astype(BF16), preferred_element_type=F32)
    d1 = jnp.sum(jnp.where(oh1, run + p1, 0.0), axis=1, keepdims=True)
    run = run + jnp.sum(oh1.astype(F32), axis=0, keepdims=True)
    run_ref[...] = jnp.broadcast_to(run, run_ref.shape)
    dd = jnp.where(lane == 0, d0, jnp.where(lane == 1, d1, 0.0))
    dest_ref[0] = dd.T[0:8, :].astype(jnp.int32)


def _plan(route, counts, *, tm=256):
    T = route.shape[0]
    return pl.pallas_call(
        functools.partial(_plan_kernel, tm=tm),
        grid=(T // tm,),
        in_specs=[pl.BlockSpec((tm, LANES), lambda i: (i, 0)), pl.BlockSpec((8, LANES), lambda i: (0, 0))],
        out_specs=pl.BlockSpec((1, 8, tm), lambda i: (i, 0, 0)),
        out_shape=jax.ShapeDtypeStruct((T // tm, 8, tm), jnp.int32),
        scratch_shapes=[pltpu.VMEM((8, LANES), F32)],
        compiler_params=_cparams("arbitrary"),
        name="moe_plan",
    )(route, counts)


def _dispatch_kernel(dest_ref, x_ref, xb_in_ref, xb_ref, sem, *, tm, nchunk):
    del xb_in_ref

    def issue(r, c):
        _row_copy(x_ref, r, xb_ref, dest_ref[0, 0, r], nchunk, sem).start()
        _row_copy(x_ref, r, xb_ref, dest_ref[0, 1, r], nchunk, sem).start()
        return c

    def drain(r, c):
        _row_copy(x_ref, r, xb_ref, 0, nchunk, sem).wait()
        _row_copy(x_ref, r, xb_ref, 0, nchunk, sem).wait()
        return c

    lax.fori_loop(0, tm, issue, 0, unroll=8)
    lax.fori_loop(0, tm, drain, 0, unroll=8)


def _dispatch(xn, dest, n_rows, *, nchunk, tm=256):
    T = xn.shape[0] // nchunk
    xb0 = jnp.zeros((n_rows * nchunk, LANES), jnp.uint32)
    return pl.pallas_call(
        functools.partial(_dispatch_kernel, tm=tm, nchunk=nchunk),
        grid=(T // tm,),
        in_specs=[pl.BlockSpec((1, 8, tm), lambda i: (i, 0, 0), memory_space=pltpu.SMEM),
                  pl.BlockSpec((tm * nchunk, LANES), lambda i: (i, 0)),
                  pl.BlockSpec(memory_space=pl.ANY)],
        out_specs=pl.BlockSpec(memory_space=pl.ANY),
        out_shape=jax.ShapeDtypeStruct((n_rows * nchunk, LANES), jnp.uint32),
        scratch_shapes=[pltpu.SemaphoreType.DMA],
        input_output_aliases={2: 0},
        compiler_params=_cparams("arbitrary"),
        name="moe_dispatch",
    )(dest, xn, xb0)


def _expert_kernel(blk_e_ref, n_used_ref, x_ref, wg_ref, wu_ref, wd_ref, o_ref, wgb_ref, wub_ref, wdb_ref, *, nchunk):
    i = pl.program_id(0)
    R = MOE_ROWS

    @pl.when((i == 0) | (blk_e_ref[i] != blk_e_ref[jnp.maximum(i - 1, 0)]))
    def _():
        wgb_ref[...] = wg_ref[0].astype(BF16)
        wub_ref[...] = wu_ref[0].astype(BF16)
        wdb_ref[...] = wd_ref[0].astype(BF16)

    @pl.when(i < n_used_ref[0])
    def _():
        hg = hu = None
        for s in range(nchunk):
            hi, lo = _unpack_f32(x_ref, s, R, nchunk)
            xs = jnp.concatenate([hi.astype(BF16), lo.astype(BF16)], axis=1)
            ks = slice(s * PACK_COLS, (s + 1) * PACK_COLS)
            pg = jnp.dot(xs, wgb_ref[ks, :], preferred_element_type=F32)
            pu = jnp.dot(xs, wub_ref[ks, :], preferred_element_type=F32)
            hg = pg if hg is None else hg + pg
            hu = pu if hu is None else hu + pu
        hid = (_silu(hg) * hu).astype(BF16)
        y = jnp.dot(hid, wdb_ref[...], preferred_element_type=F32)
        _pack_store(o_ref, y, R)

    @pl.when(i >= n_used_ref[0])
    def _():
        o_ref[...] = jnp.zeros_like(o_ref)


def _experts(xb, blk_e, n_used, w_gate, w_up, w_down, *, nchunk):
    R = MOE_ROWS
    n_rows = xb.shape[0] // nchunk
    _, D, F = w_gate.shape
    blk = pl.BlockSpec((R * nchunk, LANES), lambda i, be, nu: (i, 0))
    return pl.pallas_call(
        functools.partial(_expert_kernel, nchunk=nchunk),
        grid_spec=pltpu.PrefetchScalarGridSpec(
            num_scalar_prefetch=2,
            grid=(n_rows // R,),
            in_specs=[
                blk,
                pl.BlockSpec((1, D, F), lambda i, be, nu: (be[i], 0, 0)),
                pl.BlockSpec((1, D, F), lambda i, be, nu: (be[i], 0, 0)),
                pl.BlockSpec((1, F, D), lambda i, be, nu: (be[i], 0, 0)),
            ],
            out_specs=blk,
            scratch_shapes=[pltpu.VMEM((D, F), BF16), pltpu.VMEM((D, F), BF16), pltpu.VMEM((F, D), BF16)],
        ),
        out_shape=jax.ShapeDtypeStruct(xb.shape, jnp.uint32),
        compiler_params=pltpu.CompilerParams(dimension_semantics=("arbitrary",),
                                             vmem_limit_bytes=EXPERT_VMEM_LIMIT_BYTES),
        name="moe_experts",
    )(blk_e, n_used, xb, w_gate, w_up, w_down)


def _moe(xn, route, counts, w_gate, w_up, w_down):
    D = w_gate.shape[1]
    nchunk = D // PACK_COLS
    T = route.shape[0]
    R = MOE_ROWS
    n_rows = T * MOE_TOP_K + MOE_EXPERTS * R
    n_blocks = n_rows // R
    dest = _plan(route, counts)
    cnt = counts[0, :MOE_EXPERTS].astype(jnp.int32)
    pends = jnp.cumsum((cnt + R - 1) // R * R)
    blk_e = jnp.minimum(jnp.searchsorted(pends, jnp.arange(n_blocks, dtype=jnp.int32) * R, side='right'),
                        MOE_EXPERTS - 1).astype(jnp.int32)
    n_used = (pends[-1] // R).astype(jnp.int32).reshape(1)
    xb = _dispatch(xn, dest, n_rows, nchunk=nchunk)
    yb = _experts(xb, blk_e, n_used, w_gate, w_up, w_down, nchunk=nchunk)
    return yb, dest, route


def kernel(x, norm_mix_w, norm_ffn_w, ml_w_in, ml_b_i, ml_b_f, ml_norm_w, ml_w_out, ssd_w_in, ssd_conv_w, ssd_conv_b, ssd_dt_bias, ssd_a_log, ssd_d, ssd_norm_w, ssd_w_out, moe_w_group, moe_b_group, moe_w_expert, moe_b_expert, moe_w_gate, moe_w_up, moe_w_down, final_norm_w):
    B, S, D = x.shape
    T = B * S
    depth = norm_mix_w.shape[0]
    h = x.reshape(T, D)
    pending = None
    for layer in range(depth):
        j = layer // 2
        if layer % 2 == 0:
            H = ML_HEADS
            dv = ml_w_out.shape[1] // H
            dk = dv // 2
            main = 2 * H * dk + 2 * H * dv
            w_in = ml_w_in[j].astype(BF16)
            bias = _pad_lanes(jnp.concatenate([ml_b_i[j], ml_b_f[j]]).reshape(1, 2 * H).astype(F32))
        else:
            d_inner = ssd_w_out.shape[1]
            main = d_inner + ssd_conv_w.shape[2]
            w_in = ssd_w_in[j].astype(BF16)
            bias = _pad_lanes(ssd_dt_bias[j].reshape(1, -1).astype(F32))
        ws = _pad_lanes(w_in[:, main:])
        if pending is None:
            hn, small = _norm(h, norm_mix_w[layer], ws, bias, post="plain")
        else:
            h, hn, small = _norm(h, norm_mix_w[layer], ws, bias, combine=pending, post="plain", emit_h=True)
        proj = _matmul(hn, w_in, n_cols=main, out_dtype=BF16)
        if layer % 2 == 0:
            y = _mlstm(proj.reshape(B, S, main), small.reshape(B, S, LANES), ml_norm_w[j], dk=dk, dv=dv)
            w_out = ml_w_out[j]
        else:
            y = _ssd(proj.reshape(B, S, main), small.reshape(B, S, LANES), ssd_conv_w[j], ssd_conv_b[j],
                     ssd_a_log[j], ssd_d[j], ssd_norm_w[j], d_inner=d_inner)
            w_out = ssd_w_out[j]
        h = _matmul(y.reshape(T, -1), w_out.astype(BF16), h, out_dtype=F32)
        wr = _pad_lanes(jnp.concatenate([moe_w_group[layer], moe_w_expert[layer]], axis=1)).astype(F32)
        br = _pad_lanes(jnp.concatenate([moe_b_group[layer], moe_b_expert[layer]]).reshape(1, -1).astype(F32))
        xn, route, counts = _norm(h, norm_ffn_w[layer], wr, br, post="route")
        pending = _moe(xn, route, counts, moe_w_gate[layer], moe_w_up[layer], moe_w_down[layer])
    out = _norm(h, final_norm_w, combine=pending, out_dtype=F32)[0]
    return out.reshape(B, S, D)
```

```python
import functools

import jax
import jax.numpy as jnp
from jax import lax
from jax.experimental import pallas as pl
from jax.experimental.pallas import tpu as pltpu

F32 = jnp.float32
BF16 = jnp.bfloat16

LANES = 128
VMEM_LIMIT_BYTES = 48 * 1024 * 1024
EXPERT_VMEM_LIMIT_BYTES = 56 * 1024 * 1024

NORM_EPS = 1e-6

ML_HEADS = 4
ML_CHUNK = 256

SSM_HEAD_DIM = 64
SSM_GROUPS = 8
SSM_STATE = 128
SSM_CONV = 4
SSM_CHUNK = 128

MOE_GROUPS = 8
MOE_EXPERTS_PER_GROUP = 8
MOE_EXPERTS = MOE_GROUPS * MOE_EXPERTS_PER_GROUP
MOE_TOP_K = 2
MOE_ROWS = 256


def _cparams(*semantics):
    return pltpu.CompilerParams(dimension_semantics=semantics, vmem_limit_bytes=VMEM_LIMIT_BYTES)


def _sigmoid(x):
    return 1.0 / (1.0 + jnp.exp(-x))


def _silu(x):
    return x * _sigmoid(x)


def _softplus(x):
    return jnp.maximum(x, 0.0) + jnp.log1p(jnp.exp(-jnp.abs(x)))


def _log_sigmoid(x):
    return jnp.minimum(x, 0.0) - jnp.log1p(jnp.exp(-jnp.abs(x)))


def _tri(n):
    r = lax.broadcasted_iota(jnp.int32, (n, n), 0)
    c = lax.broadcasted_iota(jnp.int32, (n, n), 1)
    causal = r >= c
    return causal.astype(F32), causal


PACK_COLS = 2 * LANES
HI_MASK = 0xFFFF0000


def _pack_store(ref, y, rows):
    nchunk = y.shape[1] // PACK_COLS
    for s in range(nchunk):
        hi = y[:, s * PACK_COLS:s * PACK_COLS + LANES].astype(BF16).astype(F32)
        lo = y[:, s * PACK_COLS + LANES:(s + 1) * PACK_COLS].astype(BF16).astype(F32)
        word = lax.bitcast_convert_type(hi, jnp.uint32) | (lax.bitcast_convert_type(lo, jnp.uint32) >> 16)
        ref[pl.ds(s, rows, stride=nchunk), :] = word


def _unpack_f32(ref, s, rows, nchunk):
    w = ref[pl.ds(s, rows, stride=nchunk), :]
    hi = lax.bitcast_convert_type(w & jnp.uint32(HI_MASK), F32)
    lo = lax.bitcast_convert_type(w << 16, F32)
    return hi, lo


def _row_copy(src_ref, src_row, dst_ref, dst_row, nchunk, sem):
    return pltpu.make_async_copy(
        src_ref.at[pl.ds(pl.multiple_of(src_row * nchunk, nchunk), nchunk), :],
        dst_ref.at[pl.ds(pl.multiple_of(dst_row * nchunk, nchunk), nchunk), :],
        sem)


def _route(logits):
    lane = lax.broadcasted_iota(jnp.int32, logits.shape, 1)
    neg = jnp.float32(-jnp.inf)
    big = jnp.int32(4 * LANES)
    gl = jnp.where(lane < MOE_GROUPS, logits, neg)
    gmax = jnp.max(gl, axis=1, keepdims=True)
    gsum = jnp.sum(jnp.exp(gl - gmax), axis=1, keepdims=True)
    g_w = 1.0 / gsum
    g_idx = jnp.min(jnp.where(gl == gmax, lane, big), axis=1, keepdims=True)
    lo = MOE_GROUPS + g_idx * MOE_EXPERTS_PER_GROUP
    el = jnp.where((lane >= lo) & (lane < lo + MOE_EXPERTS_PER_GROUP), logits, neg)
    emax = jnp.max(el, axis=1, keepdims=True)
    esum = jnp.sum(jnp.exp(el - emax), axis=1, keepdims=True)
    idx1 = jnp.min(jnp.where(el == emax, lane, big), axis=1, keepdims=True)
    el2 = jnp.where(lane == idx1, neg, el)
    emax2 = jnp.max(el2, axis=1, keepdims=True)
    idx2 = jnp.min(jnp.where(el2 == emax2, lane, big), axis=1, keepdims=True)
    p1 = 1.0 / esum
    p2 = jnp.exp(emax2 - emax) / esum
    psum = p1 + p2
    w1 = g_w * (p1 / psum)
    w2 = g_w * (p2 / psum)
    e1 = (idx1 - MOE_GROUPS).astype(F32)
    e2 = (idx2 - MOE_GROUPS).astype(F32)
    out = jnp.where(lane == 0, w1, jnp.where(lane == 1, w2, jnp.where(lane == 2, e1, jnp.where(lane == 3, e2, 0.0))))
    return out


def _norm_kernel(*refs, combine, post, emit_h, tm):
    it = iter(refs)
    if combine:
        dest_ref, dest_next_ref, gate_ref, yb_ref = next(it), next(it), next(it), next(it)
    h_ref, nw_ref = next(it), next(it)
    ws_ref = next(it) if post else None
    b_ref = next(it) if post else None
    hnew_ref = next(it) if emit_h else None
    hn_ref = next(it)
    small_ref = next(it) if post else None
    counts_ref = next(it) if post == "route" else None
    if combine:
        y0_ref, y1_ref, sem = next(it), next(it), next(it)

    D = h_ref.shape[1]
    nchunk = D // PACK_COLS
    h = h_ref[...]
    if combine:
        step = pl.program_id(0)
        slot = step % 2

        def gather(d_ref, sl):
            def issue(r, c):
                _row_copy(yb_ref, d_ref[0, 0, r], y0_ref.at[sl], r, nchunk, sem.at[sl]).start()
                _row_copy(yb_ref, d_ref[0, 1, r], y1_ref.at[sl], r, nchunk, sem.at[sl]).start()
                return c
            lax.fori_loop(0, tm, issue, 0, unroll=8)

        @pl.when(step == 0)
        def _():
            gather(dest_ref, 0)

        @pl.when(step + 1 < pl.num_programs(0))
        def _():
            gather(dest_next_ref, 1 - slot)

        def drain(r, c):
            _row_copy(yb_ref, 0, y0_ref.at[slot], r, nchunk, sem.at[slot]).wait()
            _row_copy(yb_ref, 0, y1_ref.at[slot], r, nchunk, sem.at[slot]).wait()
            return c

        lax.fori_loop(0, tm, drain, 0, unroll=8)
        w0 = gate_ref[:, 0:1]
        w1 = gate_ref[:, 1:2]
        pieces = []
        for s in range(nchunk):
            a_hi, a_lo = _unpack_f32(y0_ref.at[slot], s, tm, nchunk)
            b_hi, b_lo = _unpack_f32(y1_ref.at[slot], s, tm, nchunk)
            pieces.append(h[:, s * PACK_COLS:s * PACK_COLS + LANES] + (w0 * a_hi + w1 * b_hi))
            pieces.append(h[:, s * PACK_COLS + LANES:(s + 1) * PACK_COLS] + (w0 * a_lo + w1 * b_lo))
        h = jnp.concatenate(pieces, axis=1)
    if emit_h:
        hnew_ref[...] = h
    y = h * lax.rsqrt(jnp.mean(h * h, axis=-1, keepdims=True) + NORM_EPS) * nw_ref[...]
    if post == "route":
        _pack_store(hn_ref, y, tm)
        s = jnp.dot(y, ws_ref[...], preferred_element_type=F32, precision=lax.Precision.HIGHEST) + b_ref[...]
        r = _route(s)
        small_ref[...] = r
        lane = lax.broadcasted_iota(jnp.int32, r.shape, 1).astype(F32)
        hits = (lane == r[:, 2:3]).astype(F32) + (lane == r[:, 3:4]).astype(F32)

        @pl.when(pl.program_id(0) == 0)
        def _():
            counts_ref[...] = jnp.zeros_like(counts_ref)

        counts_ref[...] += jnp.broadcast_to(jnp.sum(hits, axis=0, keepdims=True), counts_ref.shape)
    else:
        hn = y.astype(hn_ref.dtype)
        hn_ref[...] = hn
        if post:
            small_ref[...] = jnp.dot(hn, ws_ref[...], preferred_element_type=F32) + b_ref[...]


def _norm(h, nw, ws=None, bias=None, *, combine=None, post=None, emit_h=False, out_dtype=BF16, tm=256):
    T, D = h.shape
    nchunk = D // PACK_COLS
    row = pl.BlockSpec((tm, D), lambda i: (i, 0))
    in_specs, args, scratch = [], [], []
    if combine is not None:
        yb, dest, gate = combine
        last = T // tm - 1
        in_specs += [pl.BlockSpec((1, 8, tm), lambda i: (i, 0, 0), memory_space=pltpu.SMEM),
                     pl.BlockSpec((1, 8, tm), lambda i: (jnp.minimum(i + 1, last), 0, 0), memory_space=pltpu.SMEM),
                     pl.BlockSpec((tm, LANES), lambda i: (i, 0)),
                     pl.BlockSpec(memory_space=pl.ANY)]
        args += [dest, dest, gate, yb]
        scratch = [pltpu.VMEM((2, tm * nchunk, LANES), jnp.uint32), pltpu.VMEM((2, tm * nchunk, LANES), jnp.uint32),
                   pltpu.SemaphoreType.DMA((2,))]
    in_specs += [row, pl.BlockSpec((1, D), lambda i: (0, 0))]
    args += [h, nw.reshape(1, D).astype(F32)]
    if post:
        in_specs += [pl.BlockSpec((D, LANES), lambda i: (0, 0)), pl.BlockSpec((1, LANES), lambda i: (0, 0))]
        args += [ws, bias]
    out_shape, out_specs = [], []
    if emit_h:
        out_shape.append(jax.ShapeDtypeStruct((T, D), F32))
        out_specs.append(row)
    if post == "route":
        out_shape.append(jax.ShapeDtypeStruct((T * nchunk, LANES), jnp.uint32))
        out_specs.append(pl.BlockSpec((tm * nchunk, LANES), lambda i: (i, 0)))
    else:
        out_shape.append(jax.ShapeDtypeStruct((T, D), out_dtype))
        out_specs.append(row)
    if post:
        out_shape.append(jax.ShapeDtypeStruct((T, LANES), F32))
        out_specs.append(pl.BlockSpec((tm, LANES), lambda i: (i, 0)))
    if post == "route":
        out_shape.append(jax.ShapeDtypeStruct((8, LANES), F32))
        out_specs.append(pl.BlockSpec((8, LANES), lambda i: (0, 0)))
    return pl.pallas_call(
        functools.partial(_norm_kernel, combine=combine is not None, post=post, emit_h=emit_h, tm=tm),
        grid=(T // tm,),
        in_specs=in_specs,
        out_specs=out_specs,
        out_shape=out_shape,
        scratch_shapes=scratch,
        compiler_params=_cparams("arbitrary"),
        name="norm_" + (post or "plain") + ("_combine" if combine is not None else ""),
    )(*args)


def _pad_lanes(w):
    return jnp.pad(w, [(0, 0)] * (w.ndim - 1) + [(0, LANES - w.shape[-1])])


CONV_HIST = 16
CONV_SUB = 256
CAST_ROWS = 256


def _silu_tanh(x):
    hx = 0.5 * x
    return hx + hx * jnp.tanh(hx)


def _matmul_kernel(*refs, epilogue, tiles_per_seq):
    it = iter(refs)
    a_ref = next(it)
    ah_ref = next(it) if epilogue == "conv_silu" else None
    w_ref = next(it)
    cw_ref = next(it) if epilogue == "conv_silu" else None
    cb_ref = next(it) if epilogue == "conv_silu" else None
    r_ref = next(it) if epilogue == "res" else None
    o_ref = next(it)
    wb_ref = next(it)
    acc_refs = list(it) if epilogue == "conv_silu" else None
    i = pl.program_id(1)
    tm = a_ref.shape[0]

    @pl.when(i == 0)
    def _():
        def cast_rows(c, carry):
            rows = pl.ds(pl.multiple_of(c * CAST_ROWS, CAST_ROWS), CAST_ROWS)
            wb_ref[rows, :] = w_ref[0, rows, :].astype(BF16)
            return carry

        lax.fori_loop(0, wb_ref.shape[0] // CAST_ROWS, cast_rows, 0)

    if epilogue == "conv_silu":
        hist = ah_ref[...]
        hist = jnp.where(i % tiles_per_seq == 0, jnp.zeros_like(hist), hist)
        nsub = tm // CONV_SUB

        def sub_dot(r):
            if r == 0:
                a = jnp.concatenate([hist, a_ref[0:CONV_SUB, :]], axis=0)
            else:
                a = a_ref[r * CONV_SUB - CONV_HIST:(r + 1) * CONV_SUB, :]
            acc_refs[r][...] = jnp.dot(a, wb_ref[...], preferred_element_type=F32)

        sub_dot(0)
        for r in range(nsub):
            r0 = r * CONV_SUB
            if r + 1 < nsub:
                sub_dot(r + 1)
            acc_ref = acc_refs[r]
            out = cb_ref[...] + cw_ref[SSM_CONV - 1:SSM_CONV, :] * acc_ref[CONV_HIST:CONV_HIST + CONV_SUB, :]
            for kk in range(SSM_CONV - 1):
                sh = SSM_CONV - 1 - kk
                out = out + cw_ref[kk:kk + 1, :] * acc_ref[CONV_HIST - sh:CONV_HIST - sh + CONV_SUB, :]
            o_ref[r0:r0 + CONV_SUB, :] = _silu_tanh(out).astype(o_ref.dtype)
    else:
        acc = jnp.dot(a_ref[...], wb_ref[...], preferred_element_type=F32)
        if epilogue == "res":
            acc = acc + r_ref[...]
        elif epilogue == "silu":
            acc = _silu_tanh(acc)
        o_ref[...] = acc.astype(o_ref.dtype)


def _matmul(a, w, layer, col0, n_cols, *, out_dtype, epilogue=None, res=None, conv_w=None, conv_b=None,
            seq_len=None, tm=1024, tn=512):
    T, K = a.shape
    assert col0 % tn == 0 and n_cols % tn == 0 and T % tm == 0
    jb = col0 // tn
    in_specs, args, scratch = [pl.BlockSpec((tm, K), lambda j, i: (i, 0))], [a], [pltpu.VMEM((K, tn), BF16)]
    tiles_per_seq = None
    if epilogue == "conv_silu":
        assert seq_len % tm == 0 and tm % CONV_SUB == 0 and CONV_SUB % CONV_HIST == 0
        tiles_per_seq = seq_len // tm
        hb = tm // CONV_HIST
        in_specs.append(pl.BlockSpec((CONV_HIST, K), lambda j, i: (jnp.maximum(i * hb - 1, 0), 0)))
        args.append(a)
        scratch += [pltpu.VMEM((CONV_HIST + CONV_SUB, tn), F32) for _ in range(tm // CONV_SUB)]
    in_specs.append(pl.BlockSpec((1, K, tn), lambda j, i: (layer, 0, jb + j)))
    args.append(w)
    if epilogue == "conv_silu":
        in_specs += [pl.BlockSpec((SSM_CONV, tn), lambda j, i: (0, j)), pl.BlockSpec((1, tn), lambda j, i: (0, j))]
        args += [conv_w.astype(F32), conv_b.reshape(1, -1).astype(F32)]
    if epilogue == "res":
        in_specs.append(pl.BlockSpec((tm, tn), lambda j, i: (i, j)))
        args.append(res)
    return pl.pallas_call(
        functools.partial(_matmul_kernel, epilogue=epilogue, tiles_per_seq=tiles_per_seq),
        grid=(n_cols // tn, T // tm),
        in_specs=in_specs,
        out_specs=pl.BlockSpec((tm, tn), lambda j, i: (i, j)),
        out_shape=jax.ShapeDtypeStruct((T, n_cols), out_dtype),
        scratch_shapes=scratch,
        compiler_params=_cparams("arbitrary", "arbitrary"),
        name="matmul_" + (epilogue or "plain"),
    )(*args)


def _mlstm_kernel(proj_ref, gates_ref, nw_ref, out_ref, c_ref, n_ref, m_ref, *, dk, dv):
    H = ML_HEADS
    L = proj_ref.shape[1]
    scale = dk ** -0.5

    @pl.when(pl.program_id(1) == 0)
    def _():
        c_ref[...] = jnp.zeros_like(c_ref)
        n_ref[...] = jnp.zeros_like(n_ref)
        m_ref[...] = jnp.zeros_like(m_ref)

    tri, causal = _tri(L)
    g = gates_ref[0]
    lf = _log_sigmoid(g)
    b = jnp.dot(tri, lf, preferred_element_type=F32, precision=lax.Precision.HIGHEST)
    g_t = g.T
    b_t = b.T
    neg = jnp.float32(-jnp.inf)

    for h in range(H):
        q = proj_ref[0, :, h * dk:(h + 1) * dk]
        k = proj_ref[0, :, H * dk + h * dk:H * dk + (h + 1) * dk]
        v = proj_ref[0, :, 2 * H * dk + h * dv:2 * H * dk + (h + 1) * dv]
        o = proj_ref[0, :, 2 * H * dk + H * dv + h * dv:2 * H * dk + H * dv + (h + 1) * dv]
        icol = g[:, h:h + 1]
        bcol = b[:, H + h:H + h + 1]
        irow = g_t[h:h + 1, :]
        brow = b_t[H + h:H + h + 1, :]
        m_prev = m_ref[h, 0:1, 0:1]
        c_prev = c_ref[h]
        n_prev = n_ref[h]

        logd = jnp.where(causal, bcol - brow + irow, neg)
        inter = bcol + m_prev
        m_t = jnp.maximum(inter, jnp.max(logd, axis=1, keepdims=True))
        s = lax.dot_general(q, k, (((1,), (1,)), ((), ())), preferred_element_type=F32)
        s = s * scale * jnp.exp(logd - m_t)
        sc = jnp.exp(inter - m_t)
        qc = jnp.dot(q, c_prev.astype(BF16), preferred_element_type=F32) * scale
        num = jnp.dot(s.astype(BF16), v, preferred_element_type=F32) + sc * qc
        qn = jnp.sum(q.astype(F32) * n_prev, axis=1, keepdims=True) * scale
        den = jnp.sum(s, axis=1, keepdims=True) + sc * qn
        hout = num / jnp.maximum(jnp.abs(den), jnp.exp(-m_t))
        hn = hout * lax.rsqrt(jnp.mean(hout * hout, axis=1, keepdims=True) + NORM_EPS)
        hn = hn * nw_ref[:, h * dv:(h + 1) * dv] * _sigmoid(o.astype(F32))
        out_ref[0, :, h * dv:(h + 1) * dv] = hn.astype(out_ref.dtype)
        b_last = bcol[L - 1:L, :]
        gcol = b_last - bcol + icol
        m_new = jnp.maximum(b_last + m_prev, jnp.max(gcol, axis=0, keepdims=True))
        wk = k.astype(F32) * jnp.exp(gcol - m_new)
        dec = jnp.exp(b_last + m_prev - m_new)
        c_ref[h] = dec * c_prev + lax.dot_general(
            wk.astype(BF16), v, (((0,), (0,)), ((), ())), preferred_element_type=F32)
        n_ref[h] = dec * n_prev + jnp.sum(wk, axis=0, keepdims=True)
        m_ref[h] = jnp.broadcast_to(m_new, m_ref.shape[1:])


def _mlstm(proj, gates, norm_w, *, dk, dv):
    B, S, W = proj.shape
    H = ML_HEADS
    L = ML_CHUNK
    return pl.pallas_call(
        functools.partial(_mlstm_kernel, dk=dk, dv=dv),
        grid=(B, S // L),
        in_specs=[
            pl.BlockSpec((1, L, W), lambda b, c: (b, c, 0)),
            pl.BlockSpec((1, L, LANES), lambda b, c: (b, c, 0)),
            pl.BlockSpec((1, H * dv), lambda b, c: (0, 0)),
        ],
        out_specs=pl.BlockSpec((1, L, H * dv), lambda b, c: (b, c, 0)),
        out_shape=jax.ShapeDtypeStruct((B, S, H * dv), BF16),
        scratch_shapes=[
            pltpu.VMEM((H, dk, dv), F32),
            pltpu.VMEM((H, 1, dk), F32),
            pltpu.VMEM((H, 8, LANES), F32),
        ],
        compiler_params=_cparams("parallel", "arbitrary"),
        name="mlstm",
    )(proj, gates, norm_w.reshape(1, H * dv).astype(F32))


def _expand_heads(v, e2_ref):
    hi = v.astype(BF16)
    lo = (v - hi.astype(F32)).astype(BF16)
    return jnp.dot(jnp.concatenate([hi, lo], axis=1), e2_ref[...], preferred_element_type=F32)


def _ssd_kernel(gate_ref, u_ref, dt_ref, alog_ref, dskip_ref, nw_ref, e2_ref, out_ref, state_ref, *, d_inner):
    G, P, N = SSM_GROUPS, SSM_HEAD_DIM, SSM_STATE
    L = u_ref.shape[1]
    hpg = d_inner // (G * P)
    gw = hpg * P

    @pl.when(pl.program_id(1) == 0)
    def _():
        state_ref[...] = jnp.zeros_like(state_ref)

    tri, causal = _tri(L)
    dt = _softplus(dt_ref[0])
    a_row = -jnp.exp(alog_ref[...])
    cum = jnp.dot(tri, dt * a_row, preferred_element_type=F32, precision=lax.Precision.HIGHEST)
    cum_last = cum[L - 1:L, :]
    expcum_x = _expand_heads(jnp.exp(cum), e2_ref)
    toend_x = _expand_heads(jnp.exp(cum_last - cum) * dt, e2_ref)
    cum_t = cum.T
    dt_t = dt.T
    lane = lax.broadcasted_iota(jnp.int32, (1, LANES), 1)
    first_head = (lane < P).astype(BF16)
    second_head = (lane >= P).astype(BF16)

    for g in range(G):
        bm = u_ref[0, :, d_inner + g * N:d_inner + (g + 1) * N]
        cm = u_ref[0, :, d_inner + G * N + g * N:d_inner + G * N + (g + 1) * N]
        cs = slice(g * gw, (g + 1) * gw)
        xg = u_ref[0, :, cs]
        st = state_ref[g]
        cb = lax.dot_general(cm, bm, (((1,), (1,)), ((), ())), preferred_element_type=F32)
        cb = jnp.where(causal, cb, 0.0)
        yoff = jnp.dot(cm, st.astype(BF16), preferred_element_type=F32)
        ydiag = []
        for q in range(hpg // 2):
            wts = []
            for hd in (g * hpg + 2 * q, g * hpg + 2 * q + 1):
                seg = jnp.minimum(cum[:, hd:hd + 1] - cum_t[hd:hd + 1, :], 0.0)
                wts.append((cb * jnp.exp(seg) * dt_t[hd:hd + 1, :]).astype(BF16))
            xp = xg[:, q * LANES:(q + 1) * LANES]
            rhs = jnp.concatenate([xp * first_head, xp * second_head], axis=0)
            ydiag.append(jnp.dot(jnp.concatenate(wts, axis=1), rhs, preferred_element_type=F32))
        xf = xg.astype(F32)
        y = jnp.concatenate(ydiag, axis=1) + yoff * expcum_x[:, cs] + dskip_ref[:, cs] * xf
        xw = (xf * toend_x[:, cs]).astype(BF16)
        state_ref[g] = st * expcum_x[L - 1:L, cs] + lax.dot_general(
            bm, xw, (((0,), (0,)), ((), ())), preferred_element_type=F32)
        yg = y * gate_ref[0, :, cs].astype(F32)
        yg = yg * lax.rsqrt(jnp.mean(yg * yg, axis=1, keepdims=True) + NORM_EPS) * nw_ref[:, cs]
        out_ref[0, :, cs] = yg.astype(out_ref.dtype)


def _ssd(gate, u, dt, a_log, d_skip, norm_w):
    B, S, d_inner = gate.shape
    conv_dim = u.shape[2]
    G, P, N = SSM_GROUPS, SSM_HEAD_DIM, SSM_STATE
    L = SSM_CHUNK
    gw = d_inner // G
    heads = d_inner // P
    e2 = (jnp.arange(2 * LANES, dtype=jnp.int32)[:, None] % LANES
          == jnp.arange(heads * P, dtype=jnp.int32)[None, :] // P).astype(BF16)
    const = lambda b, c: (0, 0)
    return pl.pallas_call(
        functools.partial(_ssd_kernel, d_inner=d_inner),
        grid=(B, S // L),
        in_specs=[
            pl.BlockSpec((1, L, d_inner), lambda b, c: (b, c, 0)),
            pl.BlockSpec((1, L, conv_dim), lambda b, c: (b, c, 0)),
            pl.BlockSpec((1, L, LANES), lambda b, c: (b, c, 0)),
            pl.BlockSpec((1, LANES), const),
            pl.BlockSpec((1, d_inner), const),
            pl.BlockSpec((1, d_inner), const),
            pl.BlockSpec((2 * LANES, d_inner), const),
        ],
        out_specs=pl.BlockSpec((1, L, d_inner), lambda b, c: (b, c, 0)),
        out_shape=jax.ShapeDtypeStruct((B, S, d_inner), BF16),
        scratch_shapes=[pltpu.VMEM((G, N, gw), F32)],
        compiler_params=_cparams("parallel", "arbitrary"),
        name="ssd",
    )(gate, u, dt, _pad_lanes(a_log.reshape(1, -1).astype(F32)),
      jnp.repeat(d_skip.astype(F32), P).reshape(1, d_inner), norm_w.reshape(1, d_inner).astype(F32), e2)


def _plan_kernel(route_ref, counts_ref, dest_ref, run_ref, *, tm):
    @pl.when(pl.program_id(0) == 0)
    def _():
        lane8 = lax.broadcasted_iota(jnp.int32, (8, LANES), 1)
        padded = jnp.ceil(counts_ref[...] * (1.0 / MOE_ROWS)) * MOE_ROWS
        incl = padded
        sh = 1
        while sh < MOE_EXPERTS:
            incl = incl + jnp.where(lane8 >= sh, pltpu.roll(incl, sh, 1), 0.0)
            sh *= 2
        run_ref[...] = incl - padded

    r = route_ref[...]
    lane = lax.broadcasted_iota(jnp.int32, r.shape, 1)
    lanef = lane.astype(F32)
    oh0 = lanef == r[:, MOE_TOP_K:MOE_TOP_K + 1]
    oh1 = lanef == r[:, MOE_TOP_K + 1:MOE_TOP_K + 2]
    rr = lax.broadcasted_iota(jnp.int32, (tm, tm), 0)
    cc = lax.broadcasted_iota(jnp.int32, (tm, tm), 1)
    before = (rr > cc).astype(BF16)
    run = run_ref[0:1, :]
    p0 = jnp.dot(before, oh0.astype(BF16), preferred_element_type=F32)
    d0 = jnp.sum(jnp.where(oh0, run + p0, 0.0), axis=1, keepdims=True)
    run = run + jnp.sum(oh0.astype(F32), axis=0, keepdims=True)
    p1 = jnp.dot(before, oh1.astype(BF16), preferred_element_type=F32)
    d1 = jnp.sum(jnp.where(oh1, run + p1, 0.0), axis=1, keepdims=True)
    run = run + jnp.sum(oh1.astype(F32), axis=0, keepdims=True)
    run_ref[...] = jnp.broadcast_to(run, run_ref.shape)
    dd = jnp.where(lane == 0, d0, jnp.where(lane == 1, d1, 0.0))
    dest_ref[0] = dd.T[0:8, :].astype(jnp.int32)


def _plan(route, counts, *, tm=256):
    T = route.shape[0]
    return pl.pallas_call(
        functools.partial(_plan_kernel, tm=tm),
        grid=(T // tm,),
        in_specs=[pl.BlockSpec((tm, LANES), lambda i: (i, 0)), pl.BlockSpec((8, LANES), lambda i: (0, 0))],
        out_specs=pl.BlockSpec((1, 8, tm), lambda i: (i, 0, 0)),
        out_shape=jax.ShapeDtypeStruct((T // tm, 8, tm), jnp.int32),
        scratch_shapes=[pltpu.VMEM((8, LANES), F32)],
        compiler_params=_cparams("arbitrary"),
        name="moe_plan",
    )(route, counts)


def _dispatch_kernel(dest_ref, x_ref, xb_in_ref, xb_ref, sem, *, tm, nchunk):
    del xb_in_ref

    def issue(r, c):
        _row_copy(x_ref, r, xb_ref, dest_ref[0, 0, r], nchunk, sem).start()
        _row_copy(x_ref, r, xb_ref, dest_ref[0, 1, r], nchunk, sem).start()
        return c

    def drain(r, c):
        _row_copy(x_ref, r, xb_ref, 0, nchunk, sem).wait()
        _row_copy(x_ref, r, xb_ref, 0, nchunk, sem).wait()
        return c

    lax.fori_loop(0, tm, issue, 0, unroll=8)
    lax.fori_loop(0, tm, drain, 0, unroll=8)


def _dispatch(xn, dest, n_rows, *, nchunk, tm=256):
    T = xn.shape[0] // nchunk
    xb0 = jnp.zeros((n_rows * nchunk, LANES), jnp.uint32)
    return pl.pallas_call(
        functools.partial(_dispatch_kernel, tm=tm, nchunk=nchunk),
        grid=(T // tm,),
        in_specs=[pl.BlockSpec((1, 8, tm), lambda i: (i, 0, 0), memory_space=pltpu.SMEM),
                  pl.BlockSpec((tm * nchunk, LANES), lambda i: (i, 0)),
                  pl.BlockSpec(memory_space=pl.ANY)],
        out_specs=pl.BlockSpec(memory_space=pl.ANY),
        out_shape=jax.ShapeDtypeStruct((n_rows * nchunk, LANES), jnp.uint32),
        scratch_shapes=[pltpu.SemaphoreType.DMA],
        input_output_aliases={2: 0},
        compiler_params=_cparams("arbitrary"),
        name="moe_dispatch",
    )(dest, xn, xb0)


def _expert_kernel(blk_e_ref, n_used_ref, x_ref, wg_ref, wu_ref, wd_ref, o_ref, wgb_ref, wub_ref, wdb_ref, *, nchunk):
    i = pl.program_id(0)
    R = MOE_ROWS

    @pl.when((i == 0) | (blk_e_ref[i] != blk_e_ref[jnp.maximum(i - 1, 0)]))
    def _():
        def cast_rows(src_ref, dst_ref):
            def body(c, carry):
                rows = pl.ds(pl.multiple_of(c * CAST_ROWS, CAST_ROWS), CAST_ROWS)
                dst_ref[rows, :] = src_ref[0, rows, :].astype(BF16)
                return carry
            lax.fori_loop(0, dst_ref.shape[0] // CAST_ROWS, body, 0)

        cast_rows(wg_ref, wgb_ref)
        cast_rows(wu_ref, wub_ref)
        cast_rows(wd_ref, wdb_ref)

    @pl.when(i < n_used_ref[0])
    def _():
        hg = hu = None
        for s in range(nchunk):
            hi, lo = _unpack_f32(x_ref, s, R, nchunk)
            xs = jnp.concatenate([hi.astype(BF16), lo.astype(BF16)], axis=1)
            ks = slice(s * PACK_COLS, (s + 1) * PACK_COLS)
            pg = jnp.dot(xs, wgb_ref[ks, :], preferred_element_type=F32)
            pu = jnp.dot(xs, wub_ref[ks, :], preferred_element_type=F32)
            hg = pg if hg is None else hg + pg
            hu = pu if hu is None else hu + pu
        hid = (_silu(hg) * hu).astype(BF16)
        y = jnp.dot(hid, wdb_ref[...], preferred_element_type=F32)
        _pack_store(o_ref, y, R)

    @pl.when(i >= n_used_ref[0])
    def _():
        o_ref[...] = jnp.zeros_like(o_ref)


def _experts(xb, blk_e, n_used, w_gate, w_up, w_down, *, nchunk):
    R = MOE_ROWS
    n_rows = xb.shape[0] // nchunk
    _, D, F = w_gate.shape
    blk = pl.BlockSpec((R * nchunk, LANES), lambda i, be, nu: (i, 0))
    return pl.pallas_call(
        functools.partial(_expert_kernel, nchunk=nchunk),
        grid_spec=pltpu.PrefetchScalarGridSpec(
            num_scalar_prefetch=2,
            grid=(n_rows // R,),
            in_specs=[
                blk,
                pl.BlockSpec((1, D, F), lambda i, be, nu: (be[i], 0, 0)),
                pl.BlockSpec((1, D, F), lambda i, be, nu: (be[i], 0, 0)),
                pl.BlockSpec((1, F, D), lambda i, be, nu: (be[i], 0, 0)),
            ],
            out_specs=blk,
            scratch_shapes=[pltpu.VMEM((D, F), BF16), pltpu.VMEM((D, F), BF16), pltpu.VMEM((F, D), BF16)],
        ),
        out_shape=jax.ShapeDtypeStruct(xb.shape, jnp.uint32),
        compiler_params=pltpu.CompilerParams(dimension_semantics=("arbitrary",),
                                             vmem_limit_bytes=EXPERT_VMEM_LIMIT_BYTES),
        name="moe_experts",
    )(blk_e, n_used, xb, w_gate, w_up, w_down)


def _moe(xn, route, counts, w_gate, w_up, w_down):
    D = w_gate.shape[1]
    nchunk = D // PACK_COLS
    T = route.shape[0]
    R = MOE_ROWS
    n_rows = T * MOE_TOP_K + MOE_EXPERTS * R
    n_blocks = n_rows // R
    dest = _plan(route, counts)
    cnt = counts[0, :MOE_EXPERTS].astype(jnp.int32)
    pends = jnp.cumsum((cnt + R - 1) // R * R)
    blk_start = jnp.arange(n_blocks, dtype=jnp.int32) * R
    blk_e = jnp.minimum(jnp.sum((pends[None, :] <= blk_start[:, None]).astype(jnp.int32), axis=1), MOE_EXPERTS - 1)
    n_used = (pends[-1] // R).astype(jnp.int32).reshape(1)
    xb = _dispatch(xn, dest, n_rows, nchunk=nchunk)
    yb = _experts(xb, blk_e, n_used, w_gate, w_up, w_down, nchunk=nchunk)
    return yb, dest, route


def kernel(x, norm_mix_w, norm_ffn_w, ml_w_in, ml_b_i, ml_b_f, ml_norm_w, ml_w_out, ssd_w_in, ssd_conv_w, ssd_conv_b, ssd_dt_bias, ssd_a_log, ssd_d, ssd_norm_w, ssd_w_out, moe_w_group, moe_b_group, moe_w_expert, moe_b_expert, moe_w_gate, moe_w_up, moe_w_down, final_norm_w):
    B, S, D = x.shape
    T = B * S
    depth = norm_mix_w.shape[0]
    h = x.reshape(T, D)
    pending = None
    for layer in range(depth):
        j = layer // 2
        if layer % 2 == 0:
            H = ML_HEADS
            dv = ml_w_out.shape[1] // H
            dk = dv // 2
            main = 2 * H * dk + 2 * H * dv
            w_in = ml_w_in
            bias = _pad_lanes(jnp.concatenate([ml_b_i[j], ml_b_f[j]]).reshape(1, 2 * H).astype(F32))
        else:
            d_inner = ssd_w_out.shape[1]
            conv_dim = ssd_conv_w.shape[2]
            main = d_inner + conv_dim
            w_in = ssd_w_in
            bias = _pad_lanes(ssd_dt_bias[j].reshape(1, -1).astype(F32))
        ws = _pad_lanes(w_in[j, :, main:]).astype(BF16)
        if pending is None:
            hn, small = _norm(h, norm_mix_w[layer], ws, bias, post="plain")
        else:
            h, hn, small = _norm(h, norm_mix_w[layer], ws, bias, combine=pending, post="plain", emit_h=True)
        if layer % 2 == 0:
            proj = _matmul(hn, w_in, j, 0, main, out_dtype=BF16)
            y = _mlstm(proj.reshape(B, S, main), small.reshape(B, S, LANES), ml_norm_w[j], dk=dk, dv=dv)
            w_out, tm_out = ml_w_out, 1024
        else:
            gate = _matmul(hn, w_in, j, 0, d_inner, out_dtype=BF16, epilogue="silu")
            u = _matmul(hn, w_in, j, d_inner, conv_dim, out_dtype=BF16, epilogue="conv_silu",
                        conv_w=ssd_conv_w[j], conv_b=ssd_conv_b[j], seq_len=S)
            y = _ssd(gate.reshape(B, S, d_inner), u.reshape(B, S, conv_dim), small.reshape(B, S, LANES),
                     ssd_a_log[j], ssd_d[j], ssd_norm_w[j])
            w_out, tm_out = ssd_w_out, 512
        h = _matmul(y.reshape(T, -1), w_out, j, 0, D, out_dtype=F32, epilogue="res", res=h, tm=tm_out)
        wr = _pad_lanes(jnp.concatenate([moe_w_group[layer], moe_w_expert[layer]], axis=1)).astype(F32)
        br = _pad_lanes(jnp.concatenate([moe_b_group[layer], moe_b_expert[layer]]).reshape(1, -1).astype(F32))
        xn, route, counts = _norm(h, norm_ffn_w[layer], wr, br, post="route")
        pending = _moe(xn, route, counts, moe_w_gate[layer], moe_w_up[layer], moe_w_down[layer])
    out = _norm(h, final_norm_w, combine=pending, out_dtype=F32)[0]
    return out.reshape(B, S, D)
```

```python
import functools

import jax
import jax.numpy as jnp
from jax import lax
from jax.experimental import pallas as pl
from jax.experimental.pallas import tpu as pltpu

F32 = jnp.float32
BF16 = jnp.bfloat16

LANES = 128
VMEM_LIMIT_BYTES = 48 * 1024 * 1024
EXPERT_VMEM_LIMIT_BYTES = 56 * 1024 * 1024

NORM_EPS = 1e-6

ML_HEADS = 4
ML_CHUNK = 256

SSM_HEAD_DIM = 64
SSM_GROUPS = 8
SSM_STATE = 128
SSM_CONV = 4
SSM_CHUNK = 128

MOE_GROUPS = 8
MOE_EXPERTS_PER_GROUP = 8
MOE_EXPERTS = MOE_GROUPS * MOE_EXPERTS_PER_GROUP
MOE_TOP_K = 2
MOE_ROWS = 512
MOE_SUB = 256


def _cparams(*semantics):
    return pltpu.CompilerParams(dimension_semantics=semantics, vmem_limit_bytes=VMEM_LIMIT_BYTES)


def _sigmoid(x):
    return 1.0 / (1.0 + jnp.exp(-x))


def _silu(x):
    return x * _sigmoid(x)


def _softplus(x):
    return jnp.maximum(x, 0.0) + jnp.log1p(jnp.exp(-jnp.abs(x)))


def _log_sigmoid(x):
    return jnp.minimum(x, 0.0) - jnp.log1p(jnp.exp(-jnp.abs(x)))


def _tri(n):
    r = lax.broadcasted_iota(jnp.int32, (n, n), 0)
    c = lax.broadcasted_iota(jnp.int32, (n, n), 1)
    causal = r >= c
    return causal.astype(F32), causal


PACK_COLS = 2 * LANES
HI_MASK = 0xFFFF0000


def _pack_store(ref, y, rows, row0=0):
    nchunk = y.shape[1] // PACK_COLS
    for s in range(nchunk):
        hi = y[:, s * PACK_COLS:s * PACK_COLS + LANES].astype(BF16).astype(F32)
        lo = y[:, s * PACK_COLS + LANES:(s + 1) * PACK_COLS].astype(BF16).astype(F32)
        word = lax.bitcast_convert_type(hi, jnp.uint32) | (lax.bitcast_convert_type(lo, jnp.uint32) >> 16)
        ref[pl.ds(row0 * nchunk + s, rows, stride=nchunk), :] = word


def _unpack_f32(ref, s, rows, nchunk, row0=0):
    w = ref[pl.ds(row0 * nchunk + s, rows, stride=nchunk), :]
    hi = lax.bitcast_convert_type(w & jnp.uint32(HI_MASK), F32)
    lo = lax.bitcast_convert_type(w << 16, F32)
    return hi, lo


def _row_copy(src_ref, src_row, dst_ref, dst_row, nchunk, sem):
    return pltpu.make_async_copy(
        src_ref.at[pl.ds(pl.multiple_of(src_row * nchunk, nchunk), nchunk), :],
        dst_ref.at[pl.ds(pl.multiple_of(dst_row * nchunk, nchunk), nchunk), :],
        sem)


def _route(logits):
    lane = lax.broadcasted_iota(jnp.int32, logits.shape, 1)
    neg = jnp.float32(-jnp.inf)
    big = jnp.int32(4 * LANES)
    gl = jnp.where(lane < MOE_GROUPS, logits, neg)
    gmax = jnp.max(gl, axis=1, keepdims=True)
    gsum = jnp.sum(jnp.exp(gl - gmax), axis=1, keepdims=True)
    g_w = 1.0 / gsum
    g_idx = jnp.min(jnp.where(gl == gmax, lane, big), axis=1, keepdims=True)
    lo = MOE_GROUPS + g_idx * MOE_EXPERTS_PER_GROUP
    el = jnp.where((lane >= lo) & (lane < lo + MOE_EXPERTS_PER_GROUP), logits, neg)
    emax = jnp.max(el, axis=1, keepdims=True)
    esum = jnp.sum(jnp.exp(el - emax), axis=1, keepdims=True)
    idx1 = jnp.min(jnp.where(el == emax, lane, big), axis=1, keepdims=True)
    el2 = jnp.where(lane == idx1, neg, el)
    emax2 = jnp.max(el2, axis=1, keepdims=True)
    idx2 = jnp.min(jnp.where(el2 == emax2, lane, big), axis=1, keepdims=True)
    p1 = 1.0 / esum
    p2 = jnp.exp(emax2 - emax) / esum
    psum = p1 + p2
    w1 = g_w * (p1 / psum)
    w2 = g_w * (p2 / psum)
    e1 = (idx1 - MOE_GROUPS).astype(F32)
    e2 = (idx2 - MOE_GROUPS).astype(F32)
    out = jnp.where(lane == 0, w1, jnp.where(lane == 1, w2, jnp.where(lane == 2, e1, jnp.where(lane == 3, e2, 0.0))))
    return out


def _norm_kernel(*refs, combine, post, emit_h, tm):
    it = iter(refs)
    if combine:
        dest_ref, dest_next_ref, gate_ref, yb_ref = next(it), next(it), next(it), next(it)
    h_ref, nw_ref = next(it), next(it)
    ws_ref = next(it) if post else None
    b_ref = next(it) if post else None
    hnew_ref = next(it) if emit_h else None
    hn_ref = next(it)
    small_ref = next(it) if post else None
    counts_ref = next(it) if post == "route" else None
    if combine:
        y0_ref, y1_ref, sem = next(it), next(it), next(it)

    D = h_ref.shape[1]
    nchunk = D // PACK_COLS
    h = h_ref[...]
    if combine:
        step = pl.program_id(0)
        slot = step % 2

        def gather(d_ref, sl):
            def issue(r, c):
                _row_copy(yb_ref, d_ref[0, 0, r], y0_ref.at[sl], r, nchunk, sem.at[sl]).start(priority=0)
                _row_copy(yb_ref, d_ref[0, 1, r], y1_ref.at[sl], r, nchunk, sem.at[sl]).start(priority=1)
                return c
            lax.fori_loop(0, tm, issue, 0, unroll=8)

        @pl.when(step == 0)
        def _():
            gather(dest_ref, 0)

        @pl.when(step + 1 < pl.num_programs(0))
        def _():
            gather(dest_next_ref, 1 - slot)

        def drain(r, c):
            _row_copy(yb_ref, 0, y0_ref.at[slot], r, nchunk, sem.at[slot]).wait()
            _row_copy(yb_ref, 0, y1_ref.at[slot], r, nchunk, sem.at[slot]).wait()
            return c

        lax.fori_loop(0, tm, drain, 0, unroll=8)
        w0 = gate_ref[:, 0:1]
        w1 = gate_ref[:, 1:2]
        pieces = []
        for s in range(nchunk):
            a_hi, a_lo = _unpack_f32(y0_ref.at[slot], s, tm, nchunk)
            b_hi, b_lo = _unpack_f32(y1_ref.at[slot], s, tm, nchunk)
            pieces.append(h[:, s * PACK_COLS:s * PACK_COLS + LANES] + (w0 * a_hi + w1 * b_hi))
            pieces.append(h[:, s * PACK_COLS + LANES:(s + 1) * PACK_COLS] + (w0 * a_lo + w1 * b_lo))
        h = jnp.concatenate(pieces, axis=1)
    if emit_h:
        hnew_ref[...] = h
    y = h * lax.rsqrt(jnp.mean(h * h, axis=-1, keepdims=True) + NORM_EPS) * nw_ref[...]
    if post == "route":
        _pack_store(hn_ref, y, tm)
        y_hi = y.astype(BF16)
        y_lo = (y - y_hi.astype(F32)).astype(BF16)
        s2 = jnp.dot(y_hi, ws_ref[...], preferred_element_type=F32)
        s = (s2[:, :LANES] + s2[:, LANES:]
             + jnp.dot(y_lo, ws_ref[:, :LANES], preferred_element_type=F32) + b_ref[...])
        r = _route(s)
        small_ref[...] = r
        lane = lax.broadcasted_iota(jnp.int32, r.shape, 1).astype(F32)
        hits = (lane == r[:, 2:3]).astype(F32) + (lane == r[:, 3:4]).astype(F32)

        @pl.when(pl.program_id(0) == 0)
        def _():
            counts_ref[...] = jnp.zeros_like(counts_ref)

        counts_ref[...] += jnp.broadcast_to(jnp.sum(hits, axis=0, keepdims=True), counts_ref.shape)
    else:
        hn = y.astype(hn_ref.dtype)
        hn_ref[...] = hn
        if post:
            small_ref[...] = jnp.dot(hn, ws_ref[...], preferred_element_type=F32) + b_ref[...]


def _norm(h, nw, ws=None, bias=None, *, combine=None, post=None, emit_h=False, out_dtype=BF16, tm=256):
    T, D = h.shape
    nchunk = D // PACK_COLS
    row = pl.BlockSpec((tm, D), lambda i: (i, 0))
    in_specs, args, scratch = [], [], []
    if combine is not None:
        yb, dest, gate = combine
        last = T // tm - 1
        in_specs += [pl.BlockSpec((1, 8, tm), lambda i: (i, 0, 0), memory_space=pltpu.SMEM),
                     pl.BlockSpec((1, 8, tm), lambda i: (jnp.minimum(i + 1, last), 0, 0), memory_space=pltpu.SMEM),
                     pl.BlockSpec((tm, LANES), lambda i: (i, 0)),
                     pl.BlockSpec(memory_space=pl.ANY)]
        args += [dest, dest, gate, yb]
        scratch = [pltpu.VMEM((2, tm * nchunk, LANES), jnp.uint32), pltpu.VMEM((2, tm * nchunk, LANES), jnp.uint32),
                   pltpu.SemaphoreType.DMA((2,))]
    in_specs += [row, pl.BlockSpec((1, D), lambda i: (0, 0))]
    args += [h, nw.reshape(1, D).astype(F32)]
    if post:
        in_specs += [pl.BlockSpec(ws.shape, lambda i: (0, 0)), pl.BlockSpec((1, LANES), lambda i: (0, 0))]
        args += [ws, bias]
    out_shape, out_specs = [], []
    if emit_h:
        out_shape.append(jax.ShapeDtypeStruct((T, D), F32))
        out_specs.append(row)
    if post == "route":
        out_shape.append(jax.ShapeDtypeStruct((T * nchunk, LANES), jnp.uint32))
        out_specs.append(pl.BlockSpec((tm * nchunk, LANES), lambda i: (i, 0)))
    else:
        out_shape.append(jax.ShapeDtypeStruct((T, D), out_dtype))
        out_specs.append(row)
    if post:
        out_shape.append(jax.ShapeDtypeStruct((T, LANES), F32))
        out_specs.append(pl.BlockSpec((tm, LANES), lambda i: (i, 0)))
    if post == "route":
        out_shape.append(jax.ShapeDtypeStruct((8, LANES), F32))
        out_specs.append(pl.BlockSpec((8, LANES), lambda i: (0, 0)))
    return pl.pallas_call(
        functools.partial(_norm_kernel, combine=combine is not None, post=post, emit_h=emit_h, tm=tm),
        grid=(T // tm,),
        in_specs=in_specs,
        out_specs=out_specs,
        out_shape=out_shape,
        scratch_shapes=scratch,
        compiler_params=_cparams("arbitrary"),
        name="norm_" + (post or "plain") + ("_combine" if combine is not None else ""),
    )(*args)


def _pad_lanes(w):
    return jnp.pad(w, [(0, 0)] * (w.ndim - 1) + [(0, LANES - w.shape[-1])])


CONV_HIST = 16
CONV_SUB = 256
CAST_ROWS = 256
IN_PROJ_TN = 1024


def _silu_tanh(x):
    hx = 0.5 * x
    return hx + hx * jnp.tanh(hx)


def _matmul_kernel(*refs, epilogue, tiles_per_seq):
    it = iter(refs)
    a_ref = next(it)
    ah_ref = next(it) if epilogue == "conv_silu" else None
    w_ref = next(it)
    cw_ref = next(it) if epilogue == "conv_silu" else None
    cb_ref = next(it) if epilogue == "conv_silu" else None
    r_ref = next(it) if epilogue == "res" else None
    o_ref = next(it)
    wb_ref = next(it)
    acc_refs = list(it) if epilogue == "conv_silu" else None
    i = pl.program_id(1)
    tm = a_ref.shape[0]

    @pl.when(i == 0)
    def _():
        def cast_rows(c, carry):
            rows = pl.ds(pl.multiple_of(c * CAST_ROWS, CAST_ROWS), CAST_ROWS)
            wb_ref[rows, :] = w_ref[0, rows, :].astype(BF16)
            return carry

        lax.fori_loop(0, wb_ref.shape[0] // CAST_ROWS, cast_rows, 0)

    if epilogue == "conv_silu":
        hist = ah_ref[...]
        hist = jnp.where(i % tiles_per_seq == 0, jnp.zeros_like(hist), hist)
        nsub = tm // CONV_SUB

        def sub_dot(r):
            if r == 0:
                a = jnp.concatenate([hist, a_ref[0:CONV_SUB, :]], axis=0)
            else:
                a = a_ref[r * CONV_SUB - CONV_HIST:(r + 1) * CONV_SUB, :]
            acc_refs[r][...] = jnp.dot(a, wb_ref[...], preferred_element_type=F32)

        sub_dot(0)
        for r in range(nsub):
            r0 = r * CONV_SUB
            if r + 1 < nsub:
                sub_dot(r + 1)
            acc_ref = acc_refs[r]
            out = cb_ref[...] + cw_ref[SSM_CONV - 1:SSM_CONV, :] * acc_ref[CONV_HIST:CONV_HIST + CONV_SUB, :]
            for kk in range(SSM_CONV - 1):
                sh = SSM_CONV - 1 - kk
                out = out + cw_ref[kk:kk + 1, :] * acc_ref[CONV_HIST - sh:CONV_HIST - sh + CONV_SUB, :]
            o_ref[r0:r0 + CONV_SUB, :] = _silu_tanh(out).astype(o_ref.dtype)
    else:
        acc = jnp.dot(a_ref[...], wb_ref[...], preferred_element_type=F32)
        if epilogue == "res":
            acc = acc + r_ref[...]
        elif epilogue == "silu":
            acc = _silu_tanh(acc)
        o_ref[...] = acc.astype(o_ref.dtype)


def _matmul(a, w, layer, col0, n_cols, *, out_dtype, epilogue=None, res=None, conv_w=None, conv_b=None,
            seq_len=None, tm=1024, tn=512):
    T, K = a.shape
    assert col0 % tn == 0 and n_cols % tn == 0 and T % tm == 0
    jb = col0 // tn
    in_specs, args, scratch = [pl.BlockSpec((tm, K), lambda j, i: (i, 0))], [a], [pltpu.VMEM((K, tn), BF16)]
    tiles_per_seq = None
    if epilogue == "conv_silu":
        assert seq_len % tm == 0 and tm % CONV_SUB == 0 and CONV_SUB % CONV_HIST == 0
        tiles_per_seq = seq_len // tm
        hb = tm // CONV_HIST
        in_specs.append(pl.BlockSpec((CONV_HIST, K), lambda j, i: (jnp.maximum(i * hb - 1, 0), 0)))
        args.append(a)
        scratch += [pltpu.VMEM((CONV_HIST + CONV_SUB, tn), F32) for _ in range(tm // CONV_SUB)]
    in_specs.append(pl.BlockSpec((1, K, tn), lambda j, i: (layer, 0, jb + j)))
    args.append(w)
    if epilogue == "conv_silu":
        in_specs += [pl.BlockSpec((SSM_CONV, tn), lambda j, i: (0, j)), pl.BlockSpec((1, tn), lambda j, i: (0, j))]
        args += [conv_w.astype(F32), conv_b.reshape(1, -1).astype(F32)]
    if epilogue == "res":
        in_specs.append(pl.BlockSpec((tm, tn), lambda j, i: (i, j)))
        args.append(res)
    return pl.pallas_call(
        functools.partial(_matmul_kernel, epilogue=epilogue, tiles_per_seq=tiles_per_seq),
        grid=(n_cols // tn, T // tm),
        in_specs=in_specs,
        out_specs=pl.BlockSpec((tm, tn), lambda j, i: (i, j)),
        out_shape=jax.ShapeDtypeStruct((T, n_cols), out_dtype),
        scratch_shapes=scratch,
        compiler_params=_cparams("arbitrary", "arbitrary"),
        name="matmul_" + (epilogue or "plain"),
    )(*args)


def _mlstm_kernel(proj_ref, gates_ref, nw_ref, out_ref, c_ref, n_ref, m_ref, *, dk, dv):
    H = ML_HEADS
    L = proj_ref.shape[1]
    scale = dk ** -0.5

    @pl.when(pl.program_id(1) == 0)
    def _():
        c_ref[...] = jnp.zeros_like(c_ref)
        n_ref[...] = jnp.zeros_like(n_ref)
        m_ref[...] = jnp.zeros_like(m_ref)

    tri, causal = _tri(L)
    g = gates_ref[0]
    lf = _log_sigmoid(g)
    b = jnp.dot(tri, lf, preferred_element_type=F32, precision=lax.Precision.HIGHEST)
    g_t = g.T
    b_t = b.T
    neg = jnp.float32(-jnp.inf)

    for h in range(H):
        q = proj_ref[0, :, h * dk:(h + 1) * dk]
        k = proj_ref[0, :, H * dk + h * dk:H * dk + (h + 1) * dk]
        v = proj_ref[0, :, 2 * H * dk + h * dv:2 * H * dk + (h + 1) * dv]
        o = proj_ref[0, :, 2 * H * dk + H * dv + h * dv:2 * H * dk + H * dv + (h + 1) * dv]
        icol = g[:, h:h + 1]
        bcol = b[:, H + h:H + h + 1]
        irow = g_t[h:h + 1, :]
        brow = b_t[H + h:H + h + 1, :]
        m_prev = m_ref[h, 0:1, 0:1]
        c_prev = c_ref[h]
        n_prev = n_ref[h]

        logd = jnp.where(causal, bcol - brow + irow, neg)
        inter = bcol + m_prev
        m_t = jnp.maximum(inter, jnp.max(logd, axis=1, keepdims=True))
        s = lax.dot_general(q, k, (((1,), (1,)), ((), ())), preferred_element_type=F32)
        s = s * scale * jnp.exp(logd - m_t)
        sc = jnp.exp(inter - m_t)
        qc = jnp.dot(q, c_prev.astype(BF16), preferred_element_type=F32) * scale
        num = jnp.dot(s.astype(BF16), v, preferred_element_type=F32) + sc * qc
        qn = jnp.sum(q.astype(F32) * n_prev, axis=1, keepdims=True) * scale
        den = jnp.sum(s, axis=1, keepdims=True) + sc * qn
        hout = num / jnp.maximum(jnp.abs(den), jnp.exp(-m_t))
        hn = hout * lax.rsqrt(jnp.mean(hout * hout, axis=1, keepdims=True) + NORM_EPS)
        hn = hn * nw_ref[:, h * dv:(h + 1) * dv] * _sigmoid(o.astype(F32))
        out_ref[0, :, h * dv:(h + 1) * dv] = hn.astype(out_ref.dtype)
        b_last = bcol[L - 1:L, :]
        gcol = b_last - bcol + icol
        m_new = jnp.maximum(b_last + m_prev, jnp.max(gcol, axis=0, keepdims=True))
        wk = k.astype(F32) * jnp.exp(gcol - m_new)
        dec = jnp.exp(b_last + m_prev - m_new)
        c_ref[h] = dec * c_prev + lax.dot_general(
            wk.astype(BF16), v, (((0,), (0,)), ((), ())), preferred_element_type=F32)
        n_ref[h] = dec * n_prev + jnp.sum(wk, axis=0, keepdims=True)
        m_ref[h] = jnp.broadcast_to(m_new, m_ref.shape[1:])


def _mlstm(proj, gates, norm_w, *, dk, dv):
    B, S, W = proj.shape
    H = ML_HEADS
    L = ML_CHUNK
    return pl.pallas_call(
        functools.partial(_mlstm_kernel, dk=dk, dv=dv),
        grid=(B, S // L),
        in_specs=[
            pl.BlockSpec((1, L, W), lambda b, c: (b, c, 0)),
            pl.BlockSpec((1, L, LANES), lambda b, c: (b, c, 0)),
            pl.BlockSpec((1, H * dv), lambda b, c: (0, 0)),
        ],
        out_specs=pl.BlockSpec((1, L, H * dv), lambda b, c: (b, c, 0)),
        out_shape=jax.ShapeDtypeStruct((B, S, H * dv), BF16),
        scratch_shapes=[
            pltpu.VMEM((H, dk, dv), F32),
            pltpu.VMEM((H, 1, dk), F32),
            pltpu.VMEM((H, 8, LANES), F32),
        ],
        compiler_params=_cparams("parallel", "arbitrary"),
        name="mlstm",
    )(proj, gates, norm_w.reshape(1, H * dv).astype(F32))


def _expand_heads(v, e2_ref):
    hi = v.astype(BF16)
    lo = (v - hi.astype(F32)).astype(BF16)
    return jnp.dot(jnp.concatenate([hi, lo], axis=1), e2_ref[...], preferred_element_type=F32)


def _ssd_kernel(gate_ref, u_ref, dt_ref, alog_ref, dskip_ref, nw_ref, e2_ref, out_ref, state_ref, *, d_inner):
    G, P, N = SSM_GROUPS, SSM_HEAD_DIM, SSM_STATE
    L = u_ref.shape[1]
    hpg = d_inner // (G * P)
    gw = hpg * P

    @pl.when(pl.program_id(1) == 0)
    def _():
        state_ref[...] = jnp.zeros_like(state_ref)

    tri, causal = _tri(L)
    dt = _softplus(dt_ref[0])
    a_row = -jnp.exp(alog_ref[...])
    cum = jnp.dot(tri, dt * a_row, preferred_element_type=F32, precision=lax.Precision.HIGHEST)
    cum_last = cum[L - 1:L, :]
    expcum_x = _expand_heads(jnp.exp(cum), e2_ref)
    toend_x = _expand_heads(jnp.exp(cum_last - cum) * dt, e2_ref)
    cum_t = cum.T
    dt_t = dt.T
    lane = lax.broadcasted_iota(jnp.int32, (1, LANES), 1)
    first_head = (lane < P).astype(BF16)
    second_head = (lane >= P).astype(BF16)

    for g in range(G):
        bm = u_ref[0, :, d_inner + g * N:d_inner + (g + 1) * N]
        cm = u_ref[0, :, d_inner + G * N + g * N:d_inner + G * N + (g + 1) * N]
        cs = slice(g * gw, (g + 1) * gw)
        xg = u_ref[0, :, cs]
        st = state_ref[g]
        cb = lax.dot_general(cm, bm, (((1,), (1,)), ((), ())), preferred_element_type=F32)
        cb = jnp.where(causal, cb, 0.0)
        yoff = jnp.dot(cm, st.astype(BF16), preferred_element_type=F32)
        ydiag = []
        for q in range(hpg // 2):
            wts = []
            for hd in (g * hpg + 2 * q, g * hpg + 2 * q + 1):
                seg = jnp.minimum(cum[:, hd:hd + 1] - cum_t[hd:hd + 1, :], 0.0)
                wts.append((cb * jnp.exp(seg) * dt_t[hd:hd + 1, :]).astype(BF16))
            xp = xg[:, q * LANES:(q + 1) * LANES]
            rhs = jnp.concatenate([xp * first_head, xp * second_head], axis=0)
            ydiag.append(jnp.dot(jnp.concatenate(wts, axis=1), rhs, preferred_element_type=F32))
        xf = xg.astype(F32)
        y = jnp.concatenate(ydiag, axis=1) + yoff * expcum_x[:, cs] + dskip_ref[:, cs] * xf
        xw = (xf * toend_x[:, cs]).astype(BF16)
        state_ref[g] = st * expcum_x[L - 1:L, cs] + lax.dot_general(
            bm, xw, (((0,), (0,)), ((), ())), preferred_element_type=F32)
        yg = y * gate_ref[0, :, cs].astype(F32)
        yg = yg * lax.rsqrt(jnp.mean(yg * yg, axis=1, keepdims=True) + NORM_EPS) * nw_ref[:, cs]
        out_ref[0, :, cs] = yg.astype(out_ref.dtype)


def _ssd(gate, u, dt, a_log, d_skip, norm_w):
    B, S, d_inner = gate.shape
    conv_dim = u.shape[2]
    G, P, N = SSM_GROUPS, SSM_HEAD_DIM, SSM_STATE
    L = SSM_CHUNK
    gw = d_inner // G
    heads = d_inner // P
    e2 = (jnp.arange(2 * LANES, dtype=jnp.int32)[:, None] % LANES
          == jnp.arange(heads * P, dtype=jnp.int32)[None, :] // P).astype(BF16)
    const = lambda b, c: (0, 0)
    return pl.pallas_call(
        functools.partial(_ssd_kernel, d_inner=d_inner),
        grid=(B, S // L),
        in_specs=[
            pl.BlockSpec((1, L, d_inner), lambda b, c: (b, c, 0)),
            pl.BlockSpec((1, L, conv_dim), lambda b, c: (b, c, 0)),
            pl.BlockSpec((1, L, LANES), lambda b, c: (b, c, 0)),
            pl.BlockSpec((1, LANES), const),
            pl.BlockSpec((1, d_inner), const),
            pl.BlockSpec((1, d_inner), const),
            pl.BlockSpec((2 * LANES, d_inner), const),
        ],
        out_specs=pl.BlockSpec((1, L, d_inner), lambda b, c: (b, c, 0)),
        out_shape=jax.ShapeDtypeStruct((B, S, d_inner), BF16),
        scratch_shapes=[pltpu.VMEM((G, N, gw), F32)],
        compiler_params=_cparams("parallel", "arbitrary"),
        name="ssd",
    )(gate, u, dt, _pad_lanes(a_log.reshape(1, -1).astype(F32)),
      jnp.repeat(d_skip.astype(F32), P).reshape(1, d_inner), norm_w.reshape(1, d_inner).astype(F32), e2)


def _plan_kernel(route_ref, counts_ref, dest_ref, run_ref, *, tm):
    @pl.when(pl.program_id(0) == 0)
    def _():
        lane8 = lax.broadcasted_iota(jnp.int32, (8, LANES), 1)
        padded = jnp.ceil(counts_ref[...] * (1.0 / MOE_ROWS)) * MOE_ROWS
        incl = padded
        sh = 1
        while sh < MOE_EXPERTS:
            incl = incl + jnp.where(lane8 >= sh, pltpu.roll(incl, sh, 1), 0.0)
            sh *= 2
        run_ref[...] = incl - padded

    r = route_ref[...]
    lane = lax.broadcasted_iota(jnp.int32, r.shape, 1)
    lanef = lane.astype(F32)
    oh0 = lanef == r[:, MOE_TOP_K:MOE_TOP_K + 1]
    oh1 = lanef == r[:, MOE_TOP_K + 1:MOE_TOP_K + 2]
    rr = lax.broadcasted_iota(jnp.int32, (tm, tm), 0)
    cc = lax.broadcasted_iota(jnp.int32, (tm, tm), 1)
    before = (rr > cc).astype(BF16)
    run = run_ref[0:1, :]
    p0 = jnp.dot(before, oh0.astype(BF16), preferred_element_type=F32)
    d0 = jnp.sum(jnp.where(oh0, run + p0, 0.0), axis=1, keepdims=True)
    run = run + jnp.sum(oh0.astype(F32), axis=0, keepdims=True)
    p1 = jnp.dot(before, oh1.astype(BF16), preferred_element_type=F32)
    d1 = jnp.sum(jnp.where(oh1, run + p1, 0.0), axis=1, keepdims=True)
    run = run + jnp.sum(oh1.astype(F32), axis=0, keepdims=True)
    run_ref[...] = jnp.broadcast_to(run, run_ref.shape)
    dd = jnp.where(lane == 0, d0, jnp.where(lane == 1, d1, 0.0))
    dest_ref[0] = dd.T[0:8, :].astype(jnp.int32)


def _plan(route, counts, *, tm=256):
    T = route.shape[0]
    return pl.pallas_call(
        functools.partial(_plan_kernel, tm=tm),
        grid=(T // tm,),
        in_specs=[pl.BlockSpec((tm, LANES), lambda i: (i, 0)), pl.BlockSpec((8, LANES), lambda i: (0, 0))],
        out_specs=pl.BlockSpec((1, 8, tm), lambda i: (i, 0, 0)),
        out_shape=jax.ShapeDtypeStruct((T // tm, 8, tm), jnp.int32),
        scratch_shapes=[pltpu.VMEM((8, LANES), F32)],
        compiler_params=_cparams("arbitrary"),
        name="moe_plan",
    )(route, counts)


def _dispatch_kernel(dest_ref, x_ref, xb_in_ref, xb_ref, sem, *, tm, nchunk):
    del xb_in_ref

    def issue(r, c):
        _row_copy(x_ref, r, xb_ref, dest_ref[0, 0, r], nchunk, sem).start(priority=0)
        _row_copy(x_ref, r, xb_ref, dest_ref[0, 1, r], nchunk, sem).start(priority=1)
        return c

    def drain(r, c):
        _row_copy(x_ref, r, xb_ref, 0, nchunk, sem).wait()
        _row_copy(x_ref, r, xb_ref, 0, nchunk, sem).wait()
        return c

    lax.fori_loop(0, tm, issue, 0, unroll=8)
    lax.fori_loop(0, tm, drain, 0, unroll=8)


def _dispatch(xn, dest, n_rows, *, nchunk, tm=256):
    T = xn.shape[0] // nchunk
    xb0 = jnp.zeros((n_rows * nchunk, LANES), jnp.uint32)
    return pl.pallas_call(
        functools.partial(_dispatch_kernel, tm=tm, nchunk=nchunk),
        grid=(T // tm,),
        in_specs=[pl.BlockSpec((1, 8, tm), lambda i: (i, 0, 0), memory_space=pltpu.SMEM),
                  pl.BlockSpec((tm * nchunk, LANES), lambda i: (i, 0)),
                  pl.BlockSpec(memory_space=pl.ANY)],
        out_specs=pl.BlockSpec(memory_space=pl.ANY),
        out_shape=jax.ShapeDtypeStruct((n_rows * nchunk, LANES), jnp.uint32),
        scratch_shapes=[pltpu.SemaphoreType.DMA],
        input_output_aliases={2: 0},
        compiler_params=_cparams("arbitrary"),
        name="moe_dispatch",
    )(dest, xn, xb0)


def _expert_kernel(blk_e_ref, valid_ref, x_ref, wg_ref, wu_ref, wd_ref, o_ref, wgb_ref, wub_ref, wdb_ref, *, nchunk):
    i = pl.program_id(0)
    R = MOE_SUB

    @pl.when((i == 0) | (blk_e_ref[i] != blk_e_ref[jnp.maximum(i - 1, 0)]))
    def _():
        def cast_rows(src_ref, dst_ref):
            def body(c, carry):
                rows = pl.ds(pl.multiple_of(c * CAST_ROWS, CAST_ROWS), CAST_ROWS)
                dst_ref[rows, :] = src_ref[0, 0, rows, :].astype(BF16)
                return carry
            lax.fori_loop(0, dst_ref.shape[0] // CAST_ROWS, body, 0)

        cast_rows(wg_ref, wgb_ref)
        cast_rows(wu_ref, wub_ref)
        cast_rows(wd_ref, wdb_ref)

    def sub_block(row0):
        hg = hu = None
        for s in range(nchunk):
            hi, lo = _unpack_f32(x_ref, s, R, nchunk, row0)
            xs = jnp.concatenate([hi.astype(BF16), lo.astype(BF16)], axis=1)
            ks = slice(s * PACK_COLS, (s + 1) * PACK_COLS)
            pg = jnp.dot(xs, wgb_ref[ks, :], preferred_element_type=F32)
            pu = jnp.dot(xs, wub_ref[ks, :], preferred_element_type=F32)
            hg = pg if hg is None else hg + pg
            hu = pu if hu is None else hu + pu
        hid = (_silu(hg) * hu).astype(BF16)
        y = jnp.dot(hid, wdb_ref[...], preferred_element_type=F32)
        _pack_store(o_ref, y, R, row0)

    for sb in range(MOE_ROWS // R):
        row0 = sb * R

        @pl.when(valid_ref[i] > row0)
        def _():
            sub_block(row0)

        @pl.when(valid_ref[i] <= row0)
        def _():
            o_ref[row0 * nchunk:(row0 + R) * nchunk, :] = jnp.zeros((R * nchunk, LANES), o_ref.dtype)


def _experts(xb, blk_e, valid, w_gate, w_up, w_down, layer, *, nchunk):
    R = MOE_ROWS
    n_rows = xb.shape[0] // nchunk
    _, _, D, F = w_gate.shape
    blk = pl.BlockSpec((R * nchunk, LANES), lambda i, be, nu: (i, 0))
    return pl.pallas_call(
        functools.partial(_expert_kernel, nchunk=nchunk),
        grid_spec=pltpu.PrefetchScalarGridSpec(
            num_scalar_prefetch=2,
            grid=(n_rows // R,),
            in_specs=[
                blk,
                pl.BlockSpec((1, 1, D, F), lambda i, be, nu: (layer, be[i], 0, 0)),
                pl.BlockSpec((1, 1, D, F), lambda i, be, nu: (layer, be[i], 0, 0)),
                pl.BlockSpec((1, 1, F, D), lambda i, be, nu: (layer, be[i], 0, 0)),
            ],
            out_specs=blk,
            scratch_shapes=[pltpu.VMEM((D, F), BF16), pltpu.VMEM((D, F), BF16), pltpu.VMEM((F, D), BF16)],
        ),
        out_shape=jax.ShapeDtypeStruct(xb.shape, jnp.uint32),
        compiler_params=pltpu.CompilerParams(dimension_semantics=("arbitrary",),
                                             vmem_limit_bytes=EXPERT_VMEM_LIMIT_BYTES),
        name="moe_experts",
    )(blk_e, valid, xb, w_gate, w_up, w_down)


def _moe(xn, route, counts, w_gate, w_up, w_down, layer):
    D = w_gate.shape[2]
    nchunk = D // PACK_COLS
    T = route.shape[0]
    R = MOE_ROWS
    n_rows = T * MOE_TOP_K + MOE_EXPERTS * R
    n_blocks = n_rows // R
    dest = _plan(route, counts)
    cnt = counts[0, :MOE_EXPERTS].astype(jnp.int32)
    padded = (cnt + R - 1) // R * R
    pends = jnp.cumsum(padded)
    blk_start = jnp.arange(n_blocks, dtype=jnp.int32) * R
    blk_e = jnp.minimum(jnp.sum((pends[None, :] <= blk_start[:, None]).astype(jnp.int32), axis=1), MOE_EXPERTS - 1)
    valid = jnp.clip(cnt[blk_e] - (blk_start - (pends - padded)[blk_e]), 0, R).astype(jnp.int32)
    xb = _dispatch(xn, dest, n_rows, nchunk=nchunk)
    yb = _experts(xb, blk_e, valid, w_gate, w_up, w_down, layer, nchunk=nchunk)
    return yb, dest, route


def kernel(x, norm_mix_w, norm_ffn_w, ml_w_in, ml_b_i, ml_b_f, ml_norm_w, ml_w_out, ssd_w_in, ssd_conv_w, ssd_conv_b, ssd_dt_bias, ssd_a_log, ssd_d, ssd_norm_w, ssd_w_out, moe_w_group, moe_b_group, moe_w_expert, moe_b_expert, moe_w_gate, moe_w_up, moe_w_down, final_norm_w):
    B, S, D = x.shape
    T = B * S
    depth = norm_mix_w.shape[0]
    h = x.reshape(T, D)
    pending = None
    for layer in range(depth):
        j = layer // 2
        if layer % 2 == 0:
            H = ML_HEADS
            dv = ml_w_out.shape[1] // H
            dk = dv // 2
            main = 2 * H * dk + 2 * H * dv
            w_in = ml_w_in
            bias = _pad_lanes(jnp.concatenate([ml_b_i[j], ml_b_f[j]]).reshape(1, 2 * H).astype(F32))
        else:
            d_inner = ssd_w_out.shape[1]
            conv_dim = ssd_conv_w.shape[2]
            main = d_inner + conv_dim
            w_in = ssd_w_in
            bias = _pad_lanes(ssd_dt_bias[j].reshape(1, -1).astype(F32))
        ws = _pad_lanes(w_in[j, :, main:]).astype(BF16)
        if pending is None:
            hn, small = _norm(h, norm_mix_w[layer], ws, bias, post="plain")
        else:
            h, hn, small = _norm(h, norm_mix_w[layer], ws, bias, combine=pending, post="plain", emit_h=True)
        if layer % 2 == 0:
            proj = _matmul(hn, w_in, j, 0, main, out_dtype=BF16, tn=IN_PROJ_TN)
            y = _mlstm(proj.reshape(B, S, main), small.reshape(B, S, LANES), ml_norm_w[j], dk=dk, dv=dv)
            w_out, tm_out = ml_w_out, 1024
        else:
            gate = _matmul(hn, w_in, j, 0, d_inner, out_dtype=BF16, epilogue="silu", tn=IN_PROJ_TN)
            u = _matmul(hn, w_in, j, d_inner, conv_dim, out_dtype=BF16, epilogue="conv_silu", tn=IN_PROJ_TN,
                        conv_w=ssd_conv_w[j], conv_b=ssd_conv_b[j], seq_len=S)
            y = _ssd(gate.reshape(B, S, d_inner), u.reshape(B, S, conv_dim), small.reshape(B, S, LANES),
                     ssd_a_log[j], ssd_d[j], ssd_norm_w[j])
            w_out, tm_out = ssd_w_out, 512
        h = _matmul(y.reshape(T, -1), w_out, j, 0, D, out_dtype=F32, epilogue="res", res=h, tm=tm_out)
        wr = _pad_lanes(jnp.concatenate([moe_w_group[layer], moe_w_expert[layer]], axis=1)).astype(F32)
        wr_hi = wr.astype(BF16)
        wr = jnp.concatenate([wr_hi, (wr - wr_hi.astype(F32)).astype(BF16)], axis=1)
        br = _pad_lanes(jnp.concatenate([moe_b_group[layer], moe_b_expert[layer]]).reshape(1, -1).astype(F32))
        xn, route, counts = _norm(h, norm_ffn_w[layer], wr, br, post="route")
        pending = _moe(xn, route, counts, moe_w_gate, moe_w_up, moe_w_down, layer)
    out = _norm(h, final_norm_w, combine=pending, out_dtype=F32)[0]
    return out.reshape(B, S, D)
```

```python
import functools

import jax
import jax.numpy as jnp
from jax import lax
from jax.experimental import pallas as pl
from jax.experimental.pallas import tpu as pltpu

F32 = jnp.float32
BF16 = jnp.bfloat16

LANES = 128
VMEM_LIMIT_BYTES = 48 * 1024 * 1024
EXPERT_VMEM_LIMIT_BYTES = 56 * 1024 * 1024

NORM_EPS = 1e-6

ML_HEADS = 4
ML_CHUNK = 256

SSM_HEAD_DIM = 64
SSM_GROUPS = 8
SSM_STATE = 128
SSM_CONV = 4
SSM_CHUNK = 128

MOE_GROUPS = 8
MOE_EXPERTS_PER_GROUP = 8
MOE_EXPERTS = MOE_GROUPS * MOE_EXPERTS_PER_GROUP
MOE_TOP_K = 2
MOE_ROWS = 256


def _cparams(*semantics):
    return pltpu.CompilerParams(dimension_semantics=semantics, vmem_limit_bytes=VMEM_LIMIT_BYTES)


def _sigmoid(x):
    return 1.0 / (1.0 + jnp.exp(-x))


def _silu(x):
    return x * _sigmoid(x)


def _softplus(x):
    return jnp.maximum(x, 0.0) + jnp.log1p(jnp.exp(-jnp.abs(x)))


def _log_sigmoid(x):
    return jnp.minimum(x, 0.0) - jnp.log1p(jnp.exp(-jnp.abs(x)))


def _tri(n):
    r = lax.broadcasted_iota(jnp.int32, (n, n), 0)
    c = lax.broadcasted_iota(jnp.int32, (n, n), 1)
    causal = r >= c
    return causal.astype(F32), causal


PACK_COLS = 2 * LANES
HI_MASK = 0xFFFF0000


def _pack_store(ref, y, rows):
    nchunk = y.shape[1] // PACK_COLS
    for s in range(nchunk):
        hi = y[:, s * PACK_COLS:s * PACK_COLS + LANES].astype(BF16).astype(F32)
        lo = y[:, s * PACK_COLS + LANES:(s + 1) * PACK_COLS].astype(BF16).astype(F32)
        word = lax.bitcast_convert_type(hi, jnp.uint32) | (lax.bitcast_convert_type(lo, jnp.uint32) >> 16)
        ref[pl.ds(s, rows, stride=nchunk), :] = word


def _unpack_f32(ref, s, rows, nchunk):
    w = ref[pl.ds(s, rows, stride=nchunk), :]
    hi = lax.bitcast_convert_type(w & jnp.uint32(HI_MASK), F32)
    lo = lax.bitcast_convert_type(w << 16, F32)
    return hi, lo


def _row_copy(src_ref, src_row, dst_ref, dst_row, nchunk, sem):
    return pltpu.make_async_copy(
        src_ref.at[pl.ds(pl.multiple_of(src_row * nchunk, nchunk), nchunk), :],
        dst_ref.at[pl.ds(pl.multiple_of(dst_row * nchunk, nchunk), nchunk), :],
        sem)


def _route(logits):
    lane = lax.broadcasted_iota(jnp.int32, logits.shape, 1)
    neg = jnp.float32(-jnp.inf)
    big = jnp.int32(4 * LANES)
    gl = jnp.where(lane < MOE_GROUPS, logits, neg)
    gmax = jnp.max(gl, axis=1, keepdims=True)
    gsum = jnp.sum(jnp.exp(gl - gmax), axis=1, keepdims=True)
    g_w = 1.0 / gsum
    g_idx = jnp.min(jnp.where(gl == gmax, lane, big), axis=1, keepdims=True)
    lo = MOE_GROUPS + g_idx * MOE_EXPERTS_PER_GROUP
    el = jnp.where((lane >= lo) & (lane < lo + MOE_EXPERTS_PER_GROUP), logits, neg)
    emax = jnp.max(el, axis=1, keepdims=True)
    esum = jnp.sum(jnp.exp(el - emax), axis=1, keepdims=True)
    idx1 = jnp.min(jnp.where(el == emax, lane, big), axis=1, keepdims=True)
    el2 = jnp.where(lane == idx1, neg, el)
    emax2 = jnp.max(el2, axis=1, keepdims=True)
    idx2 = jnp.min(jnp.where(el2 == emax2, lane, big), axis=1, keepdims=True)
    p1 = 1.0 / esum
    p2 = jnp.exp(emax2 - emax) / esum
    psum = p1 + p2
    w1 = g_w * (p1 / psum)
    w2 = g_w * (p2 / psum)
    e1 = (idx1 - MOE_GROUPS).astype(F32)
    e2 = (idx2 - MOE_GROUPS).astype(F32)
    out = jnp.where(lane == 0, w1, jnp.where(lane == 1, w2, jnp.where(lane == 2, e1, jnp.where(lane == 3, e2, 0.0))))
    return out


def _norm_kernel(*refs, combine, post, emit_h, tm):
    it = iter(refs)
    if combine:
        dest_ref, dest_next_ref, gate_ref, yb_ref = next(it), next(it), next(it), next(it)
    h_ref, nw_ref = next(it), next(it)
    ws_ref = next(it) if post else None
    b_ref = next(it) if post else None
    hnew_ref = next(it) if emit_h else None
    hn_ref = next(it)
    small_ref = next(it) if post else None
    counts_ref = next(it) if post == "route" else None
    if combine:
        y0_ref, y1_ref, sem = next(it), next(it), next(it)

    D = h_ref.shape[1]
    nchunk = D // PACK_COLS
    h = h_ref[...]
    if combine:
        step = pl.program_id(0)
        slot = step % 2

        def gather(d_ref, sl):
            def issue(r, c):
                _row_copy(yb_ref, d_ref[0, 0, r], y0_ref.at[sl], r, nchunk, sem.at[sl]).start(priority=0)
                _row_copy(yb_ref, d_ref[0, 1, r], y1_ref.at[sl], r, nchunk, sem.at[sl]).start(priority=1)
                return c
            lax.fori_loop(0, tm, issue, 0, unroll=8)

        @pl.when(step == 0)
        def _():
            gather(dest_ref, 0)

        @pl.when(step + 1 < pl.num_programs(0))
        def _():
            gather(dest_next_ref, 1 - slot)

        def drain(r, c):
            _row_copy(yb_ref, 0, y0_ref.at[slot], r, nchunk, sem.at[slot]).wait()
            _row_copy(yb_ref, 0, y1_ref.at[slot], r, nchunk, sem.at[slot]).wait()
            return c

        lax.fori_loop(0, tm, drain, 0, unroll=8)
        w0 = gate_ref[:, 0:1]
        w1 = gate_ref[:, 1:2]
        pieces = []
        for s in range(nchunk):
            a_hi, a_lo = _unpack_f32(y0_ref.at[slot], s, tm, nchunk)
            b_hi, b_lo = _unpack_f32(y1_ref.at[slot], s, tm, nchunk)
            pieces.append(h[:, s * PACK_COLS:s * PACK_COLS + LANES] + (w0 * a_hi + w1 * b_hi))
            pieces.append(h[:, s * PACK_COLS + LANES:(s + 1) * PACK_COLS] + (w0 * a_lo + w1 * b_lo))
        h = jnp.concatenate(pieces, axis=1)
    if emit_h:
        hnew_ref[...] = h
    y = h * lax.rsqrt(jnp.mean(h * h, axis=-1, keepdims=True) + NORM_EPS) * nw_ref[...]
    if post == "route":
        _pack_store(hn_ref, y, tm)
        y_hi = y.astype(BF16)
        y_lo = (y - y_hi.astype(F32)).astype(BF16)
        s2 = jnp.dot(y_hi, ws_ref[...], preferred_element_type=F32)
        s = (s2[:, :LANES] + s2[:, LANES:]
             + jnp.dot(y_lo, ws_ref[:, :LANES], preferred_element_type=F32) + b_ref[...])
        r = _route(s)
        small_ref[...] = r
        lane = lax.broadcasted_iota(jnp.int32, r.shape, 1).astype(F32)
        hits = (lane == r[:, 2:3]).astype(F32) + (lane == r[:, 3:4]).astype(F32)

        @pl.when(pl.program_id(0) == 0)
        def _():
            counts_ref[...] = jnp.zeros_like(counts_ref)

        counts_ref[...] += jnp.broadcast_to(jnp.sum(hits, axis=0, keepdims=True), counts_ref.shape)
    else:
        hn = y.astype(hn_ref.dtype)
        hn_ref[...] = hn
        if post:
            small_ref[...] = jnp.dot(hn, ws_ref[...], preferred_element_type=F32) + b_ref[...]


def _norm(h, nw, ws=None, bias=None, *, combine=None, post=None, emit_h=False, out_dtype=BF16, tm=256):
    T, D = h.shape
    nchunk = D // PACK_COLS
    row = pl.BlockSpec((tm, D), lambda i: (i, 0))
    in_specs, args, scratch = [], [], []
    if combine is not None:
        yb, dest, gate = combine
        last = T // tm - 1
        in_specs += [pl.BlockSpec((1, 8, tm), lambda i: (i, 0, 0), memory_space=pltpu.SMEM),
                     pl.BlockSpec((1, 8, tm), lambda i: (jnp.minimum(i + 1, last), 0, 0), memory_space=pltpu.SMEM),
                     pl.BlockSpec((tm, LANES), lambda i: (i, 0)),
                     pl.BlockSpec(memory_space=pl.ANY)]
        args += [dest, dest, gate, yb]
        scratch = [pltpu.VMEM((2, tm * nchunk, LANES), jnp.uint32), pltpu.VMEM((2, tm * nchunk, LANES), jnp.uint32),
                   pltpu.SemaphoreType.DMA((2,))]
    in_specs += [row, pl.BlockSpec((1, D), lambda i: (0, 0))]
    args += [h, nw.reshape(1, D).astype(F32)]
    if post:
        in_specs += [pl.BlockSpec(ws.shape, lambda i: (0, 0)), pl.BlockSpec((1, LANES), lambda i: (0, 0))]
        args += [ws, bias]
    out_shape, out_specs = [], []
    if emit_h:
        out_shape.append(jax.ShapeDtypeStruct((T, D), F32))
        out_specs.append(row)
    if post == "route":
        out_shape.append(jax.ShapeDtypeStruct((T * nchunk, LANES), jnp.uint32))
        out_specs.append(pl.BlockSpec((tm * nchunk, LANES), lambda i: (i, 0)))
    else:
        out_shape.append(jax.ShapeDtypeStruct((T, D), out_dtype))
        out_specs.append(row)
    if post:
        out_shape.append(jax.ShapeDtypeStruct((T, LANES), F32))
        out_specs.append(pl.BlockSpec((tm, LANES), lambda i: (i, 0)))
    if post == "route":
        out_shape.append(jax.ShapeDtypeStruct((8, LANES), F32))
        out_specs.append(pl.BlockSpec((8, LANES), lambda i: (0, 0)))
    return pl.pallas_call(
        functools.partial(_norm_kernel, combine=combine is not None, post=post, emit_h=emit_h, tm=tm),
        grid=(T // tm,),
        in_specs=in_specs,
        out_specs=out_specs,
        out_shape=out_shape,
        scratch_shapes=scratch,
        compiler_params=_cparams("arbitrary"),
        name="norm_" + (post or "plain") + ("_combine" if combine is not None else ""),
    )(*args)


def _pad_lanes(w):
    return jnp.pad(w, [(0, 0)] * (w.ndim - 1) + [(0, LANES - w.shape[-1])])


CONV_HIST = 16
CONV_SUB = 256
CAST_ROWS = 256
IN_PROJ_TN = 1024


def _silu_tanh(x):
    hx = 0.5 * x
    return hx + hx * jnp.tanh(hx)


def _matmul_kernel(*refs, epilogue, tiles_per_seq):
    it = iter(refs)
    a_ref = next(it)
    ah_ref = next(it) if epilogue == "conv_silu" else None
    w_ref = next(it)
    cw_ref = next(it) if epilogue == "conv_silu" else None
    cb_ref = next(it) if epilogue == "conv_silu" else None
    r_ref = next(it) if epilogue == "res" else None
    o_ref = next(it)
    wb_ref = next(it)
    acc_refs = list(it) if epilogue == "conv_silu" else None
    i = pl.program_id(1)
    tm = a_ref.shape[0]

    @pl.when(i == 0)
    def _():
        def cast_rows(c, carry):
            rows = pl.ds(pl.multiple_of(c * CAST_ROWS, CAST_ROWS), CAST_ROWS)
            wb_ref[rows, :] = w_ref[0, rows, :].astype(BF16)
            return carry

        lax.fori_loop(0, wb_ref.shape[0] // CAST_ROWS, cast_rows, 0)

    if epilogue == "conv_silu":
        hist = ah_ref[...]
        hist = jnp.where(i % tiles_per_seq == 0, jnp.zeros_like(hist), hist)
        nsub = tm // CONV_SUB

        def sub_dot(r):
            if r == 0:
                a = jnp.concatenate([hist, a_ref[0:CONV_SUB, :]], axis=0)
            else:
                a = a_ref[r * CONV_SUB - CONV_HIST:(r + 1) * CONV_SUB, :]
            acc_refs[r][...] = jnp.dot(a, wb_ref[...], preferred_element_type=F32)

        sub_dot(0)
        for r in range(nsub):
            r0 = r * CONV_SUB
            if r + 1 < nsub:
                sub_dot(r + 1)
            acc_ref = acc_refs[r]
            out = cb_ref[...] + cw_ref[SSM_CONV - 1:SSM_CONV, :] * acc_ref[CONV_HIST:CONV_HIST + CONV_SUB, :]
            for kk in range(SSM_CONV - 1):
                sh = SSM_CONV - 1 - kk
                out = out + cw_ref[kk:kk + 1, :] * acc_ref[CONV_HIST - sh:CONV_HIST - sh + CONV_SUB, :]
            o_ref[r0:r0 + CONV_SUB, :] = _silu_tanh(out).astype(o_ref.dtype)
    else:
        acc = jnp.dot(a_ref[...], wb_ref[...], preferred_element_type=F32)
        if epilogue == "res":
            acc = acc + r_ref[...]
        elif epilogue == "silu":
            acc = _silu_tanh(acc)
        o_ref[...] = acc.astype(o_ref.dtype)


def _matmul(a, w, layer, col0, n_cols, *, out_dtype, epilogue=None, res=None, conv_w=None, conv_b=None,
            seq_len=None, tm=1024, tn=512):
    T, K = a.shape
    assert col0 % tn == 0 and n_cols % tn == 0 and T % tm == 0
    jb = col0 // tn
    in_specs, args, scratch = [pl.BlockSpec((tm, K), lambda j, i: (i, 0))], [a], [pltpu.VMEM((K, tn), BF16)]
    tiles_per_seq = None
    if epilogue == "conv_silu":
        assert seq_len % tm == 0 and tm % CONV_SUB == 0 and CONV_SUB % CONV_HIST == 0
        tiles_per_seq = seq_len // tm
        hb = tm // CONV_HIST
        in_specs.append(pl.BlockSpec((CONV_HIST, K), lambda j, i: (jnp.maximum(i * hb - 1, 0), 0)))
        args.append(a)
        scratch += [pltpu.VMEM((CONV_HIST + CONV_SUB, tn), F32) for _ in range(tm // CONV_SUB)]
    in_specs.append(pl.BlockSpec((1, K, tn), lambda j, i: (layer, 0, jb + j)))
    args.append(w)
    if epilogue == "conv_silu":
        in_specs += [pl.BlockSpec((SSM_CONV, tn), lambda j, i: (0, j)), pl.BlockSpec((1, tn), lambda j, i: (0, j))]
        args += [conv_w.astype(F32), conv_b.reshape(1, -1).astype(F32)]
    if epilogue == "res":
        in_specs.append(pl.BlockSpec((tm, tn), lambda j, i: (i, j)))
        args.append(res)
    return pl.pallas_call(
        functools.partial(_matmul_kernel, epilogue=epilogue, tiles_per_seq=tiles_per_seq),
        grid=(n_cols // tn, T // tm),
        in_specs=in_specs,
        out_specs=pl.BlockSpec((tm, tn), lambda j, i: (i, j)),
        out_shape=jax.ShapeDtypeStruct((T, n_cols), out_dtype),
        scratch_shapes=scratch,
        compiler_params=_cparams("arbitrary", "arbitrary"),
        name="matmul_" + (epilogue or "plain"),
    )(*args)


def _mlstm_kernel(proj_ref, gates_ref, nw_ref, out_ref, c_ref, n_ref, m_ref, *, dk, dv):
    H = ML_HEADS
    L = proj_ref.shape[1]
    scale = dk ** -0.5

    @pl.when(pl.program_id(1) == 0)
    def _():
        c_ref[...] = jnp.zeros_like(c_ref)
        n_ref[...] = jnp.zeros_like(n_ref)
        m_ref[...] = jnp.zeros_like(m_ref)

    tri, causal = _tri(L)
    g = gates_ref[0]
    lf = _log_sigmoid(g)
    b = jnp.dot(tri, lf, preferred_element_type=F32, precision=lax.Precision.HIGHEST)
    g_t = g.T
    b_t = b.T
    neg = jnp.float32(-jnp.inf)

    for h in range(H):
        q = proj_ref[0, :, h * dk:(h + 1) * dk]
        k = proj_ref[0, :, H * dk + h * dk:H * dk + (h + 1) * dk]
        v = proj_ref[0, :, 2 * H * dk + h * dv:2 * H * dk + (h + 1) * dv]
        o = proj_ref[0, :, 2 * H * dk + H * dv + h * dv:2 * H * dk + H * dv + (h + 1) * dv]
        icol = g[:, h:h + 1]
        bcol = b[:, H + h:H + h + 1]
        irow = g_t[h:h + 1, :]
        brow = b_t[H + h:H + h + 1, :]
        m_prev = m_ref[h, 0:1, 0:1]
        c_prev = c_ref[h]
        n_prev = n_ref[h]

        logd = jnp.where(causal, bcol - brow + irow, neg)
        inter = bcol + m_prev
        m_t = jnp.maximum(inter, jnp.max(logd, axis=1, keepdims=True))
        s = lax.dot_general(q, k, (((1,), (1,)), ((), ())), preferred_element_type=F32)
        s = s * scale * jnp.exp(logd - m_t)
        sc = jnp.exp(inter - m_t)
        qc = jnp.dot(q, c_prev.astype(BF16), preferred_element_type=F32) * scale
        num = jnp.dot(s.astype(BF16), v, preferred_element_type=F32) + sc * qc
        qn = jnp.sum(q.astype(F32) * n_prev, axis=1, keepdims=True) * scale
        den = jnp.sum(s, axis=1, keepdims=True) + sc * qn
        hout = num / jnp.maximum(jnp.abs(den), jnp.exp(-m_t))
        hn = hout * lax.rsqrt(jnp.mean(hout * hout, axis=1, keepdims=True) + NORM_EPS)
        hn = hn * nw_ref[:, h * dv:(h + 1) * dv] * _sigmoid(o.astype(F32))
        out_ref[0, :, h * dv:(h + 1) * dv] = hn.astype(out_ref.dtype)
        b_last = bcol[L - 1:L, :]
        gcol = b_last - bcol + icol
        m_new = jnp.maximum(b_last + m_prev, jnp.max(gcol, axis=0, keepdims=True))
        wk = k.astype(F32) * jnp.exp(gcol - m_new)
        dec = jnp.exp(b_last + m_prev - m_new)
        c_ref[h] = dec * c_prev + lax.dot_general(
            wk.astype(BF16), v, (((0,), (0,)), ((), ())), preferred_element_type=F32)
        n_ref[h] = dec * n_prev + jnp.sum(wk, axis=0, keepdims=True)
        m_ref[h] = jnp.broadcast_to(m_new, m_ref.shape[1:])


def _mlstm(proj, gates, norm_w, *, dk, dv):
    B, S, W = proj.shape
    H = ML_HEADS
    L = ML_CHUNK
    return pl.pallas_call(
        functools.partial(_mlstm_kernel, dk=dk, dv=dv),
        grid=(B, S // L),
        in_specs=[
            pl.BlockSpec((1, L, W), lambda b, c: (b, c, 0)),
            pl.BlockSpec((1, L, LANES), lambda b, c: (b, c, 0)),
            pl.BlockSpec((1, H * dv), lambda b, c: (0, 0)),
        ],
        out_specs=pl.BlockSpec((1, L, H * dv), lambda b, c: (b, c, 0)),
        out_shape=jax.ShapeDtypeStruct((B, S, H * dv), BF16),
        scratch_shapes=[
            pltpu.VMEM((H, dk, dv), F32),
            pltpu.VMEM((H, 1, dk), F32),
            pltpu.VMEM((H, 8, LANES), F32),
        ],
        compiler_params=_cparams("parallel", "arbitrary"),
        name="mlstm",
    )(proj, gates, norm_w.reshape(1, H * dv).astype(F32))


def _expand_heads(v, e2_ref):
    hi = v.astype(BF16)
    lo = (v - hi.astype(F32)).astype(BF16)
    return jnp.dot(jnp.concatenate([hi, lo], axis=1), e2_ref[...], preferred_element_type=F32)


def _ssd_kernel(gate_ref, u_ref, dt_ref, alog_ref, dskip_ref, nw_ref, e2_ref, out_ref, state_ref, *, d_inner):
    G, P, N = SSM_GROUPS, SSM_HEAD_DIM, SSM_STATE
    L = u_ref.shape[1]
    hpg = d_inner // (G * P)
    gw = hpg * P

    @pl.when(pl.program_id(1) == 0)
    def _():
        state_ref[...] = jnp.zeros_like(state_ref)

    tri, causal = _tri(L)
    dt = _softplus(dt_ref[0])
    a_row = -jnp.exp(alog_ref[...])
    cum = jnp.dot(tri, dt * a_row, preferred_element_type=F32, precision=lax.Precision.HIGHEST)
    cum_last = cum[L - 1:L, :]
    expcum_x = _expand_heads(jnp.exp(cum), e2_ref)
    toend_x = _expand_heads(jnp.exp(cum_last - cum) * dt, e2_ref)
    cum_t = cum.T
    dt_t = dt.T
    lane = lax.broadcasted_iota(jnp.int32, (1, LANES), 1)
    first_head = (lane < P).astype(BF16)
    second_head = (lane >= P).astype(BF16)

    for g in range(G):
        bm = u_ref[0, :, d_inner + g * N:d_inner + (g + 1) * N]
        cm = u_ref[0, :, d_inner + G * N + g * N:d_inner + G * N + (g + 1) * N]
        cs = slice(g * gw, (g + 1) * gw)
        xg = u_ref[0, :, cs]
        st = state_ref[g]
        cb = lax.dot_general(cm, bm, (((1,), (1,)), ((), ())), preferred_element_type=F32)
        cb = jnp.where(causal, cb, 0.0)
        yoff = jnp.dot(cm, st.astype(BF16), preferred_element_type=F32)
        ydiag = []
        for q in range(hpg // 2):
            wts = []
            for hd in (g * hpg + 2 * q, g * hpg + 2 * q + 1):
                seg = jnp.minimum(cum[:, hd:hd + 1] - cum_t[hd:hd + 1, :], 0.0)
                wts.append((cb * jnp.exp(seg) * dt_t[hd:hd + 1, :]).astype(BF16))
            xp = xg[:, q * LANES:(q + 1) * LANES]
            rhs = jnp.concatenate([xp * first_head, xp * second_head], axis=0)
            ydiag.append(jnp.dot(jnp.concatenate(wts, axis=1), rhs, preferred_element_type=F32))
        xf = xg.astype(F32)
        y = jnp.concatenate(ydiag, axis=1) + yoff * expcum_x[:, cs] + dskip_ref[:, cs] * xf
        xw = (xf * toend_x[:, cs]).astype(BF16)
        state_ref[g] = st * expcum_x[L - 1:L, cs] + lax.dot_general(
            bm, xw, (((0,), (0,)), ((), ())), preferred_element_type=F32)
        yg = y * gate_ref[0, :, cs].astype(F32)
        yg = yg * lax.rsqrt(jnp.mean(yg * yg, axis=1, keepdims=True) + NORM_EPS) * nw_ref[:, cs]
        out_ref[0, :, cs] = yg.astype(out_ref.dtype)


def _ssd(gate, u, dt, a_log, d_skip, norm_w):
    B, S, d_inner = gate.shape
    conv_dim = u.shape[2]
    G, P, N = SSM_GROUPS, SSM_HEAD_DIM, SSM_STATE
    L = SSM_CHUNK
    gw = d_inner // G
    heads = d_inner // P
    e2 = (jnp.arange(2 * LANES, dtype=jnp.int32)[:, None] % LANES
          == jnp.arange(heads * P, dtype=jnp.int32)[None, :] // P).astype(BF16)
    const = lambda b, c: (0, 0)
    return pl.pallas_call(
        functools.partial(_ssd_kernel, d_inner=d_inner),
        grid=(B, S // L),
        in_specs=[
            pl.BlockSpec((1, L, d_inner), lambda b, c: (b, c, 0)),
            pl.BlockSpec((1, L, conv_dim), lambda b, c: (b, c, 0)),
            pl.BlockSpec((1, L, LANES), lambda b, c: (b, c, 0)),
            pl.BlockSpec((1, LANES), const),
            pl.BlockSpec((1, d_inner), const),
            pl.BlockSpec((1, d_inner), const),
            pl.BlockSpec((2 * LANES, d_inner), const),
        ],
        out_specs=pl.BlockSpec((1, L, d_inner), lambda b, c: (b, c, 0)),
        out_shape=jax.ShapeDtypeStruct((B, S, d_inner), BF16),
        scratch_shapes=[pltpu.VMEM((G, N, gw), F32)],
        compiler_params=_cparams("parallel", "arbitrary"),
        name="ssd",
    )(gate, u, dt, _pad_lanes(a_log.reshape(1, -1).astype(F32)),
      jnp.repeat(d_skip.astype(F32), P).reshape(1, d_inner), norm_w.reshape(1, d_inner).astype(F32), e2)


def _plan_kernel(route_ref, counts_ref, dest_ref, run_ref, *, tm):
    @pl.when(pl.program_id(0) == 0)
    def _():
        lane8 = lax.broadcasted_iota(jnp.int32, (8, LANES), 1)
        padded = jnp.ceil(counts_ref[...] * (1.0 / MOE_ROWS)) * MOE_ROWS
        incl = padded
        sh = 1
        while sh < MOE_EXPERTS:
            incl = incl + jnp.where(lane8 >= sh, pltpu.roll(incl, sh, 1), 0.0)
            sh *= 2
        run_ref[...] = incl - padded

    r = route_ref[...]
    lane = lax.broadcasted_iota(jnp.int32, r.shape, 1)
    lanef = lane.astype(F32)
    oh0 = lanef == r[:, MOE_TOP_K:MOE_TOP_K + 1]
    oh1 = lanef == r[:, MOE_TOP_K + 1:MOE_TOP_K + 2]
    rr = lax.broadcasted_iota(jnp.int32, (tm, tm), 0)
    cc = lax.broadcasted_iota(jnp.int32, (tm, tm), 1)
    before = (rr > cc).astype(BF16)
    run = run_ref[0:1, :]
    p0 = jnp.dot(before, oh0.astype(BF16), preferred_element_type=F32)
    d0 = jnp.sum(jnp.where(oh0, run + p0, 0.0), axis=1, keepdims=True)
    run = run + jnp.sum(oh0.astype(F32), axis=0, keepdims=True)
    p1 = jnp.dot(before, oh1.astype(BF16), preferred_element_type=F32)
    d1 = jnp.sum(jnp.where(oh1, run + p1, 0.0), axis=1, keepdims=True)
    run = run + jnp.sum(oh1.astype(F32), axis=0, keepdims=True)
    run_ref[...] = jnp.broadcast_to(run, run_ref.shape)
    dd = jnp.where(lane == 0, d0, jnp.where(lane == 1, d1, 0.0))
    dest_ref[0] = dd.T[0:8, :].astype(jnp.int32)


def _plan(route, counts, *, tm=256):
    T = route.shape[0]
    return pl.pallas_call(
        functools.partial(_plan_kernel, tm=tm),
        grid=(T // tm,),
        in_specs=[pl.BlockSpec((tm, LANES), lambda i: (i, 0)), pl.BlockSpec((8, LANES), lambda i: (0, 0))],
        out_specs=pl.BlockSpec((1, 8, tm), lambda i: (i, 0, 0)),
        out_shape=jax.ShapeDtypeStruct((T // tm, 8, tm), jnp.int32),
        scratch_shapes=[pltpu.VMEM((8, LANES), F32)],
        compiler_params=_cparams("arbitrary"),
        name="moe_plan",
    )(route, counts)


def _dispatch_kernel(dest_ref, x_ref, xb_in_ref, xb_ref, sem, *, tm, nchunk):
    del xb_in_ref

    def issue(r, c):
        _row_copy(x_ref, r, xb_ref, dest_ref[0, 0, r], nchunk, sem).start(priority=0)
        _row_copy(x_ref, r, xb_ref, dest_ref[0, 1, r], nchunk, sem).start(priority=1)
        return c

    def drain(r, c):
        _row_copy(x_ref, r, xb_ref, 0, nchunk, sem).wait()
        _row_copy(x_ref, r, xb_ref, 0, nchunk, sem).wait()
        return c

    lax.fori_loop(0, tm, issue, 0, unroll=8)
    lax.fori_loop(0, tm, drain, 0, unroll=8)


def _dispatch(xn, dest, n_rows, *, nchunk, tm=256):
    T = xn.shape[0] // nchunk
    xb0 = jnp.zeros((n_rows * nchunk, LANES), jnp.uint32)
    return pl.pallas_call(
        functools.partial(_dispatch_kernel, tm=tm, nchunk=nchunk),
        grid=(T // tm,),
        in_specs=[pl.BlockSpec((1, 8, tm), lambda i: (i, 0, 0), memory_space=pltpu.SMEM),
                  pl.BlockSpec((tm * nchunk, LANES), lambda i: (i, 0)),
                  pl.BlockSpec(memory_space=pl.ANY)],
        out_specs=pl.BlockSpec(memory_space=pl.ANY),
        out_shape=jax.ShapeDtypeStruct((n_rows * nchunk, LANES), jnp.uint32),
        scratch_shapes=[pltpu.SemaphoreType.DMA],
        input_output_aliases={2: 0},
        compiler_params=_cparams("arbitrary"),
        name="moe_dispatch",
    )(dest, xn, xb0)


def _expert_kernel(blk_e_ref, n_used_ref, next_e_ref, run_ref, x_ref, wg_ref, wu_ref, wd_ref, o_ref,
                   wgb_ref, wub_ref, wdb_ref, wgf_ref, wuf_ref, wdf_ref, sem, *, nchunk, layer):
    i = pl.program_id(0)
    R = MOE_ROWS
    e = blk_e_ref[i]
    slot = run_ref[i] % 2

    def fetch(ex, sl):
        return (pltpu.make_async_copy(wg_ref.at[layer, ex], wgf_ref.at[sl], sem.at[sl]),
                pltpu.make_async_copy(wu_ref.at[layer, ex], wuf_ref.at[sl], sem.at[sl]),
                pltpu.make_async_copy(wd_ref.at[layer, ex], wdf_ref.at[sl], sem.at[sl]))

    @pl.when(i == 0)
    def _():
        for c in fetch(e, 0):
            c.start()

    @pl.when((i == 0) | (e != blk_e_ref[jnp.maximum(i - 1, 0)]))
    def _():
        for c in fetch(e, slot):
            c.wait()

        @pl.when(next_e_ref[i] != e)
        def _():
            for c in fetch(next_e_ref[i], 1 - slot):
                c.start()

        def cast_rows(src_ref, dst_ref):
            def body(c, carry):
                rows = pl.ds(pl.multiple_of(c * CAST_ROWS, CAST_ROWS), CAST_ROWS)
                dst_ref[rows, :] = src_ref[slot, rows, :].astype(BF16)
                return carry
            lax.fori_loop(0, dst_ref.shape[0] // CAST_ROWS, body, 0)

        cast_rows(wgf_ref, wgb_ref)
        cast_rows(wuf_ref, wub_ref)
        cast_rows(wdf_ref, wdb_ref)

    @pl.when(i < n_used_ref[0])
    def _():
        hg = hu = None
        for s in range(nchunk):
            hi, lo = _unpack_f32(x_ref, s, R, nchunk)
            xs = jnp.concatenate([hi.astype(BF16), lo.astype(BF16)], axis=1)
            ks = slice(s * PACK_COLS, (s + 1) * PACK_COLS)
            pg = jnp.dot(xs, wgb_ref[ks, :], preferred_element_type=F32)
            pu = jnp.dot(xs, wub_ref[ks, :], preferred_element_type=F32)
            hg = pg if hg is None else hg + pg
            hu = pu if hu is None else hu + pu
        hid = (_silu(hg) * hu).astype(BF16)
        y = jnp.dot(hid, wdb_ref[...], preferred_element_type=F32)
        _pack_store(o_ref, y, R)

    @pl.when(i >= n_used_ref[0])
    def _():
        o_ref[...] = jnp.zeros_like(o_ref)


def _experts(xb, blk_e, n_used, w_gate, w_up, w_down, layer, *, nchunk):
    R = MOE_ROWS
    n_rows = xb.shape[0] // nchunk
    _, _, D, F = w_gate.shape
    blk = pl.BlockSpec((R * nchunk, LANES), lambda i, *_: (i, 0))
    later = jnp.where(blk_e[None, :] > blk_e[:, None], blk_e[None, :], MOE_EXPERTS)
    next_e = jnp.min(later, axis=1)
    next_e = jnp.where(next_e == MOE_EXPERTS, blk_e, next_e).astype(jnp.int32)
    run = jnp.cumsum(jnp.concatenate([jnp.zeros((1,), jnp.int32),
                                      (blk_e[1:] != blk_e[:-1]).astype(jnp.int32)])).astype(jnp.int32)
    hbm = pl.BlockSpec(memory_space=pl.ANY)
    return pl.pallas_call(
        functools.partial(_expert_kernel, nchunk=nchunk, layer=layer),
        grid_spec=pltpu.PrefetchScalarGridSpec(
            num_scalar_prefetch=4,
            grid=(n_rows // R,),
            in_specs=[blk, hbm, hbm, hbm],
            out_specs=blk,
            scratch_shapes=[pltpu.VMEM((D, F), BF16), pltpu.VMEM((D, F), BF16), pltpu.VMEM((F, D), BF16),
                            pltpu.VMEM((2, D, F), F32), pltpu.VMEM((2, D, F), F32), pltpu.VMEM((2, F, D), F32),
                            pltpu.SemaphoreType.DMA((2,))],
        ),
        out_shape=jax.ShapeDtypeStruct(xb.shape, jnp.uint32),
        compiler_params=pltpu.CompilerParams(dimension_semantics=("arbitrary",),
                                             vmem_limit_bytes=EXPERT_VMEM_LIMIT_BYTES),
        name="moe_experts",
    )(blk_e, n_used, next_e, run, xb, w_gate, w_up, w_down)


def _moe(xn, route, counts, w_gate, w_up, w_down, layer):
    D = w_gate.shape[2]
    nchunk = D // PACK_COLS
    T = route.shape[0]
    R = MOE_ROWS
    n_rows = T * MOE_TOP_K + MOE_EXPERTS * R
    n_blocks = n_rows // R
    dest = _plan(route, counts)
    cnt = counts[0, :MOE_EXPERTS].astype(jnp.int32)
    pends = jnp.cumsum((cnt + R - 1) // R * R)
    blk_start = jnp.arange(n_blocks, dtype=jnp.int32) * R
    blk_e = jnp.minimum(jnp.sum((pends[None, :] <= blk_start[:, None]).astype(jnp.int32), axis=1), MOE_EXPERTS - 1)
    n_used = (pends[-1] // R).astype(jnp.int32).reshape(1)
    xb = _dispatch(xn, dest, n_rows, nchunk=nchunk)
    yb = _experts(xb, blk_e, n_used, w_gate, w_up, w_down, layer, nchunk=nchunk)
    return yb, dest, route


def kernel(x, norm_mix_w, norm_ffn_w, ml_w_in, ml_b_i, ml_b_f, ml_norm_w, ml_w_out, ssd_w_in, ssd_conv_w, ssd_conv_b, ssd_dt_bias, ssd_a_log, ssd_d, ssd_norm_w, ssd_w_out, moe_w_group, moe_b_group, moe_w_expert, moe_b_expert, moe_w_gate, moe_w_up, moe_w_down, final_norm_w):
    B, S, D = x.shape
    T = B * S
    depth = norm_mix_w.shape[0]
    h = x.reshape(T, D)
    pending = None
    for layer in range(depth):
        j = layer // 2
        if layer % 2 == 0:
            H = ML_HEADS
            dv = ml_w_out.shape[1] // H
            dk = dv // 2
            main = 2 * H * dk + 2 * H * dv
            w_in = ml_w_in
            bias = _pad_lanes(jnp.concatenate([ml_b_i[j], ml_b_f[j]]).reshape(1, 2 * H).astype(F32))
        else:
            d_inner = ssd_w_out.shape[1]
            conv_dim = ssd_conv_w.shape[2]
            main = d_inner + conv_dim
            w_in = ssd_w_in
            bias = _pad_lanes(ssd_dt_bias[j].reshape(1, -1).astype(F32))
        ws = _pad_lanes(w_in[j, :, main:]).astype(BF16)
        if pending is None:
            hn, small = _norm(h, norm_mix_w[layer], ws, bias, post="plain")
        else:
            h, hn, small = _norm(h, norm_mix_w[layer], ws, bias, combine=pending, post="plain", emit_h=True)
        if layer % 2 == 0:
            proj = _matmul(hn, w_in, j, 0, main, out_dtype=BF16, tn=IN_PROJ_TN)
            y = _mlstm(proj.reshape(B, S, main), small.reshape(B, S, LANES), ml_norm_w[j], dk=dk, dv=dv)
            w_out, tm_out = ml_w_out, 1024
        else:
            gate = _matmul(hn, w_in, j, 0, d_inner, out_dtype=BF16, epilogue="silu", tn=IN_PROJ_TN)
            u = _matmul(hn, w_in, j, d_inner, conv_dim, out_dtype=BF16, epilogue="conv_silu", tn=IN_PROJ_TN,
                        conv_w=ssd_conv_w[j], conv_b=ssd_conv_b[j], seq_len=S)
            y = _ssd(gate.reshape(B, S, d_inner), u.reshape(B, S, conv_dim), small.reshape(B, S, LANES),
                     ssd_a_log[j], ssd_d[j], ssd_norm_w[j])
            w_out, tm_out = ssd_w_out, 512
        h = _matmul(y.reshape(T, -1), w_out, j, 0, D, out_dtype=F32, epilogue="res", res=h, tm=tm_out)
        wr = _pad_lanes(jnp.concatenate([moe_w_group[layer], moe_w_expert[layer]], axis=1)).astype(F32)
        wr_hi = wr.astype(BF16)
        wr = jnp.concatenate([wr_hi, (wr - wr_hi.astype(F32)).astype(BF16)], axis=1)
        br = _pad_lanes(jnp.concatenate([moe_b_group[layer], moe_b_expert[layer]]).reshape(1, -1).astype(F32))
        xn, route, counts = _norm(h, norm_ffn_w[layer], wr, br, post="route")
        pending = _moe(xn, route, counts, moe_w_gate, moe_w_up, moe_w_down, layer)
    out = _norm(h, final_norm_w, combine=pending, out_dtype=F32)[0]
    return out.reshape(B, S, D)
```
